```python
import jax
import jax.numpy as jnp
from jax import lax
import numpy as np

D_MODEL = 2048
BATCH = 4
SEQ = 2048
DEPTH = 1
DEC_BATCH = 128
DEC_SEQ = 8
PAST_LEN = 16384
PAGE_SIZE = 128

CONV_CH = D_MODEL // 2
CONV_WIDTH = 31
GLA_HEADS = 4
GLA_DK = D_MODEL // 2 // GLA_HEADS
GLA_DV = D_MODEL // GLA_HEADS
GLA_KEY = GLA_HEADS * GLA_DK
GLA_VAL = GLA_HEADS * GLA_DV
GATE_RANK = 16
GATE_TEMP = 16.0
GLA_CHUNK = 64
PEER_HEADS = 8
PEER_NKEYS = 128
PEER_EXPERTS = PEER_NKEYS * PEER_NKEYS
PEER_QDIM = 256
PEER_TOPK = 16
PEER_BLOCK = 256
EPS = 1e-6

PROJ_SPLITS = (2 * CONV_CH, GLA_KEY, GLA_KEY, GLA_VAL, GLA_VAL, GATE_RANK, D_MODEL, D_MODEL)
PROJ_COLS = sum(PROJ_SPLITS)

kernel_name = 'conv_gla_gated_peer_step'


def _split_cols(z):
    idx = [int(i) for i in np.cumsum(PROJ_SPLITS)[:-1]]
    return jnp.split(z, idx, axis=-1)


def _rmsnorm(x, g):
    xf = x.astype(jnp.float32)
    y = xf * lax.rsqrt(jnp.mean(xf * xf, axis=-1, keepdims=True) + EPS)
    return (y * g.astype(jnp.float32)).astype(x.dtype)


def _layernorm(x, g, b):
    xf = x.astype(jnp.float32)
    mu = jnp.mean(xf, axis=-1, keepdims=True)
    var = jnp.mean(jnp.square(xf - mu), axis=-1, keepdims=True)
    y = (xf - mu) * lax.rsqrt(var + EPS)
    return (y * g.astype(jnp.float32) + b.astype(jnp.float32)).astype(x.dtype)


def _conv_branch(zc, conv_state, w_dw, b_dw, ln_g, ln_b, w_conv_out):
    a, gate = jnp.split(zc, 2, axis=-1)
    u = a * jax.nn.sigmoid(gate)
    ext = jnp.concatenate([conv_state.astype(u.dtype), u], axis=1)
    c = lax.conv_general_dilated(ext, w_dw[:, None, :].astype(u.dtype), window_strides=(1,), padding='VALID', dimension_numbers=('NWC', 'WIO', 'NWC'), feature_group_count=CONV_CH) + b_dw
    c = jax.nn.silu(_layernorm(c, ln_g, ln_b))
    return c @ w_conv_out, ext[:, ext.shape[1] - (CONV_WIDTH - 1):, :]


def _gla_recurrence(q, k, v, lg, s0):
    bsz, seq = q.shape[0], q.shape[1]
    blk = min(GLA_CHUNK, seq)
    n_blk = -(-seq // blk)
    seq_p = n_blk * blk

    def to_blocks(t):
        t = jnp.pad(t.astype(jnp.float32), ((0, 0), (0, seq_p - seq), (0, 0), (0, 0)))
        return t.reshape(bsz, n_blk, blk, t.shape[2], t.shape[3]).transpose(1, 0, 3, 2, 4)

    causal = jnp.tril(jnp.ones((blk, blk), dtype=bool))[:, :, None]

    def step(s, inp):
        qc, kc, vc, lc = inp
        b = jnp.cumsum(lc, axis=2)
        o_inter = jnp.einsum('bhtd,bhde->bhte', qc * jnp.exp(b), s)
        diff = b[:, :, :, None, :] - b[:, :, None, :, :]
        decay = jnp.exp(jnp.where(causal, diff, -jnp.inf))
        att = jnp.einsum('bhtd,bhsd,bhtsd->bhts', qc, kc, decay)
        o = o_inter + jnp.einsum('bhts,bhse->bhte', att, vc)
        b_last = b[:, :, -1:, :]
        s_new = jnp.exp(b_last[:, :, 0, :])[..., None] * s + jnp.einsum('bhsd,bhse->bhde', kc * jnp.exp(b_last - b), vc)
        return s_new, o

    s_fin, o = lax.scan(step, s0.astype(jnp.float32), (to_blocks(q), to_blocks(k), to_blocks(v), to_blocks(lg)))
    o = o.transpose(1, 0, 3, 2, 4).reshape(bsz, seq_p, GLA_HEADS, GLA_DV)[:, :seq]
    return o, s_fin


def _peer(h, wq, k1, k2, u_tab, v_tab):
    bsz, seq, dm = h.shape
    t = bsz * seq
    hf = h.reshape(t, dm)
    q = (hf @ wq).astype(jnp.float32).reshape(t, PEER_HEADS, 2, PEER_QDIM // 2)
    s1 = jnp.einsum('thd,nd->thn', q[:, :, 0], k1.astype(jnp.float32))
    s2 = jnp.einsum('thd,nd->thn', q[:, :, 1], k2.astype(jnp.float32))
    v1, i1 = lax.top_k(s1, PEER_TOPK)
    v2, i2 = lax.top_k(s2, PEER_TOPK)
    cand = (v1[..., :, None] + v2[..., None, :]).reshape(t, PEER_HEADS, PEER_TOPK * PEER_TOPK)
    cs, ci = lax.top_k(cand, PEER_TOPK)
    e = jnp.take_along_axis(i1, ci // PEER_TOPK, axis=-1) * PEER_NKEYS + jnp.take_along_axis(i2, ci % PEER_TOPK, axis=-1)
    g = jax.nn.softmax(cs, axis=-1)
    e = e.reshape(t, PEER_HEADS * PEER_TOPK)
    g = g.reshape(t, PEER_HEADS * PEER_TOPK).astype(h.dtype)
    blk = min(PEER_BLOCK, t)
    n_blk = -(-t // blk)
    pad = n_blk * blk - t
    hp = jnp.pad(hf, ((0, pad), (0, 0))).reshape(n_blk, blk, dm)
    ep = jnp.pad(e, ((0, pad), (0, 0))).reshape(n_blk, blk, PEER_HEADS * PEER_TOPK)
    gp = jnp.pad(g, ((0, pad), (0, 0))).reshape(n_blk, blk, PEER_HEADS * PEER_TOPK)

    def block(args):
        hb, eb, gb = args
        act = jax.nn.gelu(jnp.einsum('tkd,td->tk', u_tab[eb], hb))
        return jnp.einsum('tk,tkd->td', gb * act, v_tab[eb])

    out = lax.map(block, (hp, ep, gp)).reshape(n_blk * blk, dm)[:t]
    return out.reshape(bsz, seq, dm)


def _layer(x, conv_state, gla_state, norm1_g, w_in, w_a2, b_a, w_dw, b_dw, conv_ln_g, conv_ln_b, w_conv_out, gla_norm_g, w_gla_o, w_mix_out, norm2_g, peer_wq, peer_k1, peer_k2, peer_u, peer_v):
    bsz, seq = x.shape[0], x.shape[1]
    h = _rmsnorm(x, norm1_g)
    zc, zq, zk, zv, zr, za, zga, zgb = _split_cols(h @ w_in)
    a_out, conv_new = _conv_branch(zc, conv_state, w_dw, b_dw, conv_ln_g, conv_ln_b, w_conv_out)
    q = zq.reshape(bsz, seq, GLA_HEADS, GLA_DK) * (GLA_DK ** -0.5)
    k = zk.reshape(bsz, seq, GLA_HEADS, GLA_DK)
    v = zv.reshape(bsz, seq, GLA_HEADS, GLA_DV)
    lg = (jax.nn.log_sigmoid((za @ w_a2 + b_a).astype(jnp.float32)) / GATE_TEMP).reshape(bsz, seq, GLA_HEADS, GLA_DK)
    o, gla_new = _gla_recurrence(q, k, v, lg, gla_state)
    o = o * lax.rsqrt(jnp.mean(o * o, axis=-1, keepdims=True) + EPS)
    o = (o.reshape(bsz, seq, GLA_VAL) * gla_norm_g.astype(jnp.float32)).astype(x.dtype) * jax.nn.silu(zr)
    b_out = o @ w_gla_o
    merged = jax.nn.sigmoid(zga) * a_out + jax.nn.sigmoid(zgb) * b_out
    x = x + merged @ w_mix_out
    x = x + _peer(_rmsnorm(x, norm2_g), peer_wq, peer_k1, peer_k2, peer_u, peer_v)
    return x, conv_new, gla_new


def setup_inputs(seed: int = 0) -> dict:
    key = jax.random.key(seed)
    ks = jax.random.split(key, 23)

    def nrm(k, shape, scale):
        return jax.random.normal(k, shape, jnp.float32) * scale

    return {
        'x_prompt': nrm(ks[0], (BATCH, SEQ, D_MODEL), 1.0),
        'x_sample': nrm(ks[1], (DEC_BATCH, DEC_SEQ, D_MODEL), 1.0),
        'state_conv': nrm(ks[2], (DEPTH, DEC_BATCH, CONV_WIDTH - 1, CONV_CH), 0.5),
        'state_gla': nrm(ks[3], (DEPTH, DEC_BATCH, GLA_HEADS, GLA_DK, GLA_DV), 0.1),
        'norm1_g': 1.0 + nrm(ks[4], (DEPTH, D_MODEL), 0.02),
        'w_in': nrm(ks[5], (DEPTH, D_MODEL, PROJ_COLS), D_MODEL ** -0.5),
        'w_a2': nrm(ks[6], (DEPTH, GATE_RANK, GLA_KEY), GATE_RANK ** -0.5),
        'b_a': nrm(ks[7], (DEPTH, GLA_KEY), 0.01),
        'w_dw': nrm(ks[8], (DEPTH, CONV_WIDTH, CONV_CH), CONV_WIDTH ** -0.5),
        'b_dw': nrm(ks[9], (DEPTH, CONV_CH), 0.01),
        'conv_ln_g': 1.0 + nrm(ks[10], (DEPTH, CONV_CH), 0.02),
        'conv_ln_b': nrm(ks[11], (DEPTH, CONV_CH), 0.01),
        'w_conv_out': nrm(ks[12], (DEPTH, CONV_CH, D_MODEL), CONV_CH ** -0.5),
        'gla_norm_g': 1.0 + nrm(ks[13], (DEPTH, GLA_VAL), 0.02),
        'w_gla_o': nrm(ks[14], (DEPTH, GLA_VAL, D_MODEL), GLA_VAL ** -0.5),
        'w_mix_out': nrm(ks[15], (DEPTH, D_MODEL, D_MODEL), D_MODEL ** -0.5),
        'norm2_g': 1.0 + nrm(ks[16], (DEPTH, D_MODEL), 0.02),
        'peer_wq': nrm(ks[17], (DEPTH, D_MODEL, PEER_HEADS * PEER_QDIM), D_MODEL ** -0.5),
        'peer_k1': nrm(ks[18], (DEPTH, PEER_NKEYS, PEER_QDIM // 2), (PEER_QDIM // 2) ** -0.5),
        'peer_k2': nrm(ks[19], (DEPTH, PEER_NKEYS, PEER_QDIM // 2), (PEER_QDIM // 2) ** -0.5),
        'peer_u': nrm(ks[20], (DEPTH, PEER_EXPERTS, D_MODEL), D_MODEL ** -0.5),
        'peer_v': nrm(ks[21], (DEPTH, PEER_EXPERTS, D_MODEL), 0.5),
        'final_g': 1.0 + nrm(ks[22], (D_MODEL,), 0.02),
    }


def reference(x_prompt, x_sample, state_conv, state_gla, norm1_g, w_in, w_a2, b_a, w_dw, b_dw, conv_ln_g, conv_ln_b, w_conv_out, gla_norm_g, w_gla_o, w_mix_out, norm2_g, peer_wq, peer_k1, peer_k2, peer_u, peer_v, final_g):
    xp = x_prompt
    xs = x_sample
    nb = x_prompt.shape[0]
    conv_p, gla_p, conv_s, gla_s = [], [], [], []
    for l in range(DEPTH):
        params = (norm1_g[l], w_in[l], w_a2[l], b_a[l], w_dw[l], b_dw[l], conv_ln_g[l], conv_ln_b[l], w_conv_out[l], gla_norm_g[l], w_gla_o[l], w_mix_out[l], norm2_g[l], peer_wq[l], peer_k1[l], peer_k2[l], peer_u[l], peer_v[l])
        zero_conv = jnp.zeros((nb, CONV_WIDTH - 1, CONV_CH), x_prompt.dtype)
        zero_gla = jnp.zeros((nb, GLA_HEADS, GLA_DK, GLA_DV), jnp.float32)
        xp, cp, sp = _layer(xp, zero_conv, zero_gla, *params)
        xs, cs, ss = _layer(xs, state_conv[l], state_gla[l], *params)
        conv_p.append(cp.astype(state_conv.dtype))
        gla_p.append(sp.astype(state_gla.dtype))
        conv_s.append(cs.astype(state_conv.dtype))
        gla_s.append(ss.astype(state_gla.dtype))
    y_prompt = _rmsnorm(xp, final_g)
    y_sample = _rmsnorm(xs, final_g)
    return (y_prompt, y_sample, jnp.stack(conv_p), jnp.stack(gla_p), jnp.stack(conv_s), jnp.stack(gla_s))
```

```python
import functools
import math

import numpy as np
import jax
import jax.numpy as jnp
from jax import lax
from jax.experimental import pallas as pl
from jax.experimental.pallas import tpu as pltpu

EPS = 1e-6
GLA_CHUNK = 64
GATE_TEMP = 16.0
PEER_TOPK = 16
CONV_HIST_ROWS = 32
LANES = 128
VMEM_LIMIT_BYTES = 56 * 1024 * 1024

BF16 = jnp.bfloat16
F32 = jnp.float32


def _tile(n, pref, mult=8):
    if n <= pref:
        return n
    for t in range(pref - pref % mult, 0, -mult):
        if n % t == 0:
            return t
    return n


def _params(*sem):
    return pltpu.CompilerParams(dimension_semantics=sem, vmem_limit_bytes=VMEM_LIMIT_BYTES)


def _dot(a, b):
    return jnp.dot(a, b, preferred_element_type=F32)


def _dot_tn(a, b):
    return lax.dot_general(a, b, (((0,), (0,)), ((), ())), preferred_element_type=F32)


def _dot_nt(a, b):
    return lax.dot_general(a, b, (((1,), (1,)), ((), ())), preferred_element_type=F32)


def _split3(x):
    hi = x.astype(BF16)
    r1 = x - hi.astype(F32)
    mid = r1.astype(BF16)
    lo = (r1 - mid.astype(F32)).astype(BF16)
    return hi, mid, lo


def _rmsnorm_kernel(x_ref, g_ref, o_ref):
    x = x_ref[...]
    y = x * lax.rsqrt(jnp.mean(x * x, axis=-1, keepdims=True) + EPS)
    o_ref[...] = (y * g_ref[...]).astype(o_ref.dtype)


def _rmsnorm(x, g, out_dtype):
    t, d = x.shape
    tm = _tile(t, 512)
    return pl.pallas_call(
        _rmsnorm_kernel,
        grid=(t // tm,),
        in_specs=[pl.BlockSpec((tm, d), lambda i: (i, 0)), pl.BlockSpec((1, d), lambda i: (0, 0))],
        out_specs=pl.BlockSpec((tm, d), lambda i: (i, 0)),
        out_shape=jax.ShapeDtypeStruct((t, d), out_dtype),
        compiler_params=_params("parallel"),
        name="rmsnorm",
    )(x, g.reshape(1, d))


def _glu_kernel(h_ref, wa_ref, wg_ref, o_ref):
    h = h_ref[...]
    o_ref[...] = _dot(h, wa_ref[...]) * jax.nn.sigmoid(_dot(h, wg_ref[...]))


def _glu_proj(h, wa, wg):
    t, k = h.shape
    n = wa.shape[1]
    tm, tn = _tile(t, 1024), _tile(n, 512, LANES)
    return pl.pallas_call(
        _glu_kernel,
        grid=(t // tm, n // tn),
        in_specs=[pl.BlockSpec((tm, k), lambda i, j: (i, 0)),
                  pl.BlockSpec((k, tn), lambda i, j: (0, j)),
                  pl.BlockSpec((k, tn), lambda i, j: (0, j))],
        out_specs=pl.BlockSpec((tm, tn), lambda i, j: (i, j)),
        out_shape=jax.ShapeDtypeStruct((t, n), F32),
        compiler_params=_params("parallel", "parallel"),
        name="glu_proj",
    )(h, wa, wg)


def _linear_kernel(h_ref, w_ref, o_ref):
    o_ref[...] = _dot(h_ref[...], w_ref[...])


def _linear(h, w):
    t, k = h.shape
    n = w.shape[1]
    tm, tn = _tile(t, 1024), _tile(n, 1024, LANES)
    return pl.pallas_call(
        _linear_kernel,
        grid=(t // tm, n // tn),
        in_specs=[pl.BlockSpec((tm, k), lambda i, j: (i, 0)), pl.BlockSpec((k, tn), lambda i, j: (0, j))],
        out_specs=pl.BlockSpec((tm, tn), lambda i, j: (i, j)),
        out_shape=jax.ShapeDtypeStruct((t, n), F32),
        compiler_params=_params("parallel", "parallel"),
        name="qkv_proj",
    )(h, w)


def _act_linear_kernel(h_ref, w_ref, o_ref, *, n_silu_tiles):
    z = _dot(h_ref[...], w_ref[...])
    sig = jax.nn.sigmoid(z)
    o_ref[...] = jnp.where(pl.program_id(1) < n_silu_tiles, z * sig, sig).astype(o_ref.dtype)


def _act_linear(h, w, n_silu):
    t, k = h.shape
    n = w.shape[1]
    tm, tn = _tile(t, 1024), _tile(math.gcd(n, n_silu), 1024, LANES)
    return pl.pallas_call(
        functools.partial(_act_linear_kernel, n_silu_tiles=n_silu // tn),
        grid=(t // tm, n // tn),
        in_specs=[pl.BlockSpec((tm, k), lambda i, j: (i, 0)), pl.BlockSpec((k, tn), lambda i, j: (0, j))],
        out_specs=pl.BlockSpec((tm, tn), lambda i, j: (i, j)),
        out_shape=jax.ShapeDtypeStruct((t, n), BF16),
        compiler_params=_params("parallel", "parallel"),
        name="gate_proj",
    )(h, w)


def _decay_kernel(h_ref, wa_ref, wa2_ref, ba_ref, o_ref):
    za = _dot(h_ref[...], wa_ref[...])
    x = jnp.dot(za, wa2_ref[...], preferred_element_type=F32, precision=lax.Precision.HIGHEST) + ba_ref[...]
    o_ref[...] = (jnp.minimum(x, 0.0) - jnp.log1p(jnp.exp(-jnp.abs(x)))) * (1.0 / GATE_TEMP)


def _decay_proj(h, wa, wa2, ba):
    t, k = h.shape
    rp = wa.shape[1]
    n = wa2.shape[1]
    tm = _tile(t, 512)
    return pl.pallas_call(
        _decay_kernel,
        grid=(t // tm,),
        in_specs=[pl.BlockSpec((tm, k), lambda i: (i, 0)),
                  pl.BlockSpec((k, rp), lambda i: (0, 0)),
                  pl.BlockSpec((rp, n), lambda i: (0, 0)),
                  pl.BlockSpec((1, n), lambda i: (0, 0))],
        out_specs=pl.BlockSpec((tm, n), lambda i: (i, 0)),
        out_shape=jax.ShapeDtypeStruct((t, n), F32),
        compiler_params=_params("parallel"),
        name="decay_proj",
    )(h, wa, wa2, ba.reshape(1, n))


def _conv_kernel(ext_ref, w_ref, b_ref, g_ref, beta_ref, o_ref, *, nb, lt, width, rt):
    off = CONV_HIST_ROWS - (width - 1)
    bias, gain, beta = b_ref[...], g_ref[...], beta_ref[...]

    def row_tile(n, r0):
        acc = ext_ref[n, pl.ds(r0 + off, rt), :] * w_ref[0:1, :]
        for j in range(1, width):
            acc = acc + ext_ref[n, pl.ds(r0 + off + j, rt), :] * w_ref[j:j + 1, :]
        c = acc + bias
        mu = jnp.mean(c, axis=-1, keepdims=True)
        d = c - mu
        var = jnp.mean(d * d, axis=-1, keepdims=True)
        y = d * lax.rsqrt(var + EPS) * gain + beta
        o_ref[n, pl.ds(r0, rt), :] = y * jax.nn.sigmoid(y)

    for n in range(nb):
        for r0 in range(0, lt, rt):
            row_tile(n, r0)


def _conv_branch(ext, w_dw, b_dw, ln_g, ln_b, lt, nb):
    n, rows, c = ext.shape
    width = w_dw.shape[0]
    rt = _tile(lt, 32)
    row = lambda a: a.reshape(1, c)
    return pl.pallas_call(
        functools.partial(_conv_kernel, nb=nb, lt=lt, width=width, rt=rt),
        grid=(n // nb,),
        in_specs=[pl.BlockSpec((nb, rows, c), lambda i: (i, 0, 0)),
                  pl.BlockSpec((width, c), lambda i: (0, 0)),
                  pl.BlockSpec((1, c), lambda i: (0, 0)),
                  pl.BlockSpec((1, c), lambda i: (0, 0)),
                  pl.BlockSpec((1, c), lambda i: (0, 0))],
        out_specs=pl.BlockSpec((nb, lt, c), lambda i: (i, 0, 0)),
        out_shape=jax.ShapeDtypeStruct((n, lt, c), F32),
        compiler_params=_params("parallel"),
        name="conv_branch",
    )(ext, w_dw, row(b_dw), row(ln_g), row(ln_b))


def _conv_windows(u_seq, hist, lt):
    n, l, c = u_seq.shape
    front = jnp.zeros((n, CONV_HIST_ROWS - hist.shape[1], c), u_seq.dtype)
    full = jnp.concatenate([front, hist, u_seq], axis=1)
    wins = [full[:, i * lt:i * lt + CONV_HIST_ROWS + lt] for i in range(l // lt)]
    return jnp.stack(wins, axis=1).reshape(n * (l // lt), CONV_HIST_ROWS + lt, c), full


def _gla_tables(c):
    nlev = int(math.log2(c))
    assert 2 ** nlev == c
    t = np.arange(c)
    tri = (t[None, :] <= t[:, None]).astype(np.float32)
    mats = [tri]
    masks = []
    for l in range(nlev):
        hs = 2 ** l
        mid = (t // (2 * hs)) * (2 * hs) + hs
        mats.append(tri[mid - 1])
        same = (t[:, None] // (2 * hs)) == (t[None, :] // (2 * hs))
        upper = (t % (2 * hs)) >= hs
        masks.append((same & upper[:, None] & ~upper[None, :]).astype(np.float32))
    mats.append(np.ones((c, c), np.float32))
    return jnp.asarray(np.concatenate(mats, axis=0), BF16), jnp.asarray(np.stack(masks)), nlev


def _gla_kernel(q_ref, k_ref, v_ref, lg_ref, r_ref, s0_ref, gn_ref, mall_ref, masks_ref, o_ref, sout_ref, s_scr,
                *, nb, c, nlev, scale):
    chunk = pl.program_id(2)

    @pl.when(chunk == 0)
    def _():
        s_scr[...] = s0_ref[:, 0]

    dk = q_ref.shape[1]
    dv = v_ref.shape[1]
    mall = mall_ref[...]
    gate = r_ref[...].astype(F32)
    row_id = lax.broadcasted_iota(jnp.int32, (c, dk), 0)
    eye = lax.broadcasted_iota(jnp.int32, (c, c), 0) == lax.broadcasted_iota(jnp.int32, (c, c), 1)
    ones = jnp.ones((c, LANES), BF16)
    outs = []
    for n in range(nb):
        rows = slice(n * c, (n + 1) * c)
        q = q_ref[rows, :] * scale
        k = k_ref[rows, :]
        v = v_ref[rows, :].astype(BF16)
        parts = _split3(lg_ref[rows, :])
        ball = _dot(mall, parts[0]) + _dot(mall, parts[1]) + _dot(mall, parts[2])
        b = ball[0:c]
        b_last = ball[(nlev + 1) * c:(nlev + 2) * c]
        att = jnp.where(eye, jnp.sum(q * k, axis=-1, keepdims=True), 0.0)
        for l in range(nlev):
            ref = ball[(l + 1) * c:(l + 2) * c]
            upper = (row_id & (2 ** (l + 1) - 1)) >= 2 ** l
            x = (jnp.where(upper, q, k) * jnp.exp(-jnp.abs(b - ref))).astype(BF16)
            att = att + _dot_nt(x, x) * masks_ref[l]
        s = s_scr[n]
        o = _dot(att.astype(BF16), v) + _dot((q * jnp.exp(b)).astype(BF16), s.astype(BF16))
        kd = (k * jnp.exp(b_last - b)).astype(BF16)
        b_col = _dot_tn(parts[0], ones) + _dot_tn(parts[1], ones) + _dot_tn(parts[2], ones)
        decay = jnp.concatenate([jnp.exp(b_col)] * (dv // LANES), axis=1)
        s_scr[n] = decay * s + _dot_tn(kd, v)
        on = o * lax.rsqrt(jnp.mean(o * o, axis=-1, keepdims=True) + EPS)
        outs.append(on * gn_ref[...] * gate[rows, :])
    o_ref[...] = (outs[0] if nb == 1 else jnp.concatenate(outs, axis=0)).astype(o_ref.dtype)

    @pl.when(chunk == pl.num_programs(2) - 1)
    def _():
        sout_ref[:, 0] = s_scr[...]


def _gla_branch(zqkv, lg, zrgg, s0, gn, *, row0, seq_len, nb):
    nseq, heads, dk, dv = s0.shape
    c = min(GLA_CHUNK, seq_len)
    nc = seq_len // c
    assert seq_len % c == 0 and (nb == 1 or nc == 1) and nseq % nb == 0
    rows = nb * c
    assert row0 % rows == 0 and (2 * heads * dk) % dv == 0 and dv % LANES == 0
    base = row0 // rows
    voff = 2 * heads * dk // dv
    mall, masks, nlev = _gla_tables(c)
    rmap = lambda s, h, ch: base + s * nc + ch
    t_out = nseq * seq_len
    return pl.pallas_call(
        functools.partial(_gla_kernel, nb=nb, c=c, nlev=nlev, scale=dk ** -0.5),
        grid=(nseq // nb, heads, nc),
        in_specs=[pl.BlockSpec((rows, dk), lambda s, h, ch: (rmap(s, h, ch), h)),
                  pl.BlockSpec((rows, dk), lambda s, h, ch: (rmap(s, h, ch), heads + h)),
                  pl.BlockSpec((rows, dv), lambda s, h, ch: (rmap(s, h, ch), voff + h)),
                  pl.BlockSpec((rows, dk), lambda s, h, ch: (rmap(s, h, ch), h)),
                  pl.BlockSpec((rows, dv), lambda s, h, ch: (rmap(s, h, ch), h)),
                  pl.BlockSpec((nb, 1, dk, dv), lambda s, h, ch: (s, h, 0, 0)),
                  pl.BlockSpec((1, dv), lambda s, h, ch: (0, h)),
                  pl.BlockSpec(mall.shape, lambda s, h, ch: (0, 0)),
                  pl.BlockSpec(masks.shape, lambda s, h, ch: (0, 0, 0))],
        out_specs=[pl.BlockSpec((rows, dv), lambda s, h, ch: (s * nc + ch, h)),
                   pl.BlockSpec((nb, 1, dk, dv), lambda s, h, ch: (s, h, 0, 0))],
        out_shape=[jax.ShapeDtypeStruct((t_out, heads * dv), BF16),
                   jax.ShapeDtypeStruct((nseq, heads, dk, dv), F32)],
        scratch_shapes=[pltpu.VMEM((nb, dk, dv), F32)],
        compiler_params=_params("parallel", "parallel", "arbitrary"),
        name="gla_branch",
    )(zqkv, zqkv, zqkv, lg, zrgg, s0, gn.reshape(1, heads * dv), mall, masks)


def _merge_kernel(c_ref, og_ref, wc_ref, wg_ref, ga_ref, gb_ref, o_ref):
    a = _dot(c_ref[...].astype(BF16), wc_ref[...])
    b = _dot(og_ref[...], wg_ref[...])
    o_ref[...] = (ga_ref[...].astype(F32) * a + gb_ref[...].astype(F32) * b).astype(o_ref.dtype)


def _merge(cact, og, wc, wg, zrgg, ga_col0, gb_col0):
    t, kc = cact.shape
    kg = og.shape[1]
    n = wc.shape[1]
    tm, tn = _tile(t, 512), _tile(math.gcd(math.gcd(n, ga_col0), gb_col0), 1024, LANES)
    return pl.pallas_call(
        _merge_kernel,
        grid=(t // tm, n // tn),
        in_specs=[pl.BlockSpec((tm, kc), lambda i, j: (i, 0)),
                  pl.BlockSpec((tm, kg), lambda i, j: (i, 0)),
                  pl.BlockSpec((kc, tn), lambda i, j: (0, j)),
                  pl.BlockSpec((kg, tn), lambda i, j: (0, j)),
                  pl.BlockSpec((tm, tn), lambda i, j: (i, ga_col0 // tn + j)),
                  pl.BlockSpec((tm, tn), lambda i, j: (i, gb_col0 // tn + j))],
        out_specs=pl.BlockSpec((tm, tn), lambda i, j: (i, j)),
        out_shape=jax.ShapeDtypeStruct((t, n), BF16),
        compiler_params=_params("parallel", "parallel"),
        name="merge_proj",
    )(cact, og, wc, wg, zrgg, zrgg)


def _mix_kernel(m_ref, w_ref, x_ref, g_ref, x2_ref, h2_ref):
    x2 = x_ref[...] + _dot(m_ref[...], w_ref[...])
    x2_ref[...] = x2
    y = x2 * lax.rsqrt(jnp.mean(x2 * x2, axis=-1, keepdims=True) + EPS)
    h2_ref[...] = (y * g_ref[...]).astype(h2_ref.dtype)


def _mix(merged, w, x, g):
    t, d = x.shape
    k = merged.shape[1]
    tm = _tile(t, 256)
    return pl.pallas_call(
        _mix_kernel,
        grid=(t // tm,),
        in_specs=[pl.BlockSpec((tm, k), lambda i: (i, 0)),
                  pl.BlockSpec((k, d), lambda i: (0, 0)),
                  pl.BlockSpec((tm, d), lambda i: (i, 0)),
                  pl.BlockSpec((1, d), lambda i: (0, 0))],
        out_specs=[pl.BlockSpec((tm, d), lambda i: (i, 0)), pl.BlockSpec((tm, d), lambda i: (i, 0))],
        out_shape=[jax.ShapeDtypeStruct((t, d), F32), jax.ShapeDtypeStruct((t, d), BF16)],
        compiler_params=_params("parallel"),
        name="mix_proj",
    )(merged, w, x, g.reshape(1, d))


def _score_kernel(h2t_ref, wqt_ref, k1_ref, k2_ref, s_ref, *, heads, half):
    qt = _dot(wqt_ref[...], h2t_ref[...])
    k1, k2 = k1_ref[...], k2_ref[...]
    for h in range(heads):
        lo = 2 * half * h
        s_ref[2 * h] = _dot(k1, qt[lo:lo + half].astype(BF16))
        s_ref[2 * h + 1] = _dot(k2, qt[lo + half:lo + 2 * half].astype(BF16))


def _peer_scores(h2t, wqt, k1, k2, heads):
    d, t = h2t.shape
    nq = wqt.shape[0]
    nkeys, half = k1.shape
    tl = _tile(t, 512, LANES)
    return pl.pallas_call(
        functools.partial(_score_kernel, heads=heads, half=half),
        grid=(t // tl,),
        in_specs=[pl.BlockSpec((d, tl), lambda i: (0, i)),
                  pl.BlockSpec((nq, d), lambda i: (0, 0)),
                  pl.BlockSpec((nkeys, half), lambda i: (0, 0)),
                  pl.BlockSpec((nkeys, half), lambda i: (0, 0))],
        out_specs=pl.BlockSpec((2 * heads, nkeys, tl), lambda i: (0, 0, i)),
        out_shape=jax.ShapeDtypeStruct((2 * heads, nkeys, t), F32),
        compiler_params=_params("parallel"),
        name="peer_scores",
    )(h2t, wqt, k1, k2)


def _top_ranks(s, k):
    n, tl = s.shape
    idx = lax.broadcasted_iota(jnp.int32, (n, tl), 0).astype(F32)
    slot = lax.broadcasted_iota(jnp.int32, (k, tl), 0)

    def body(r, carry):
        work, rank, vals = carry
        m = jnp.max(work, axis=0, keepdims=True)
        first = jnp.min(jnp.where(work == m, idx, float(n)), axis=0, keepdims=True)
        sel = idx == first
        rank = jnp.where(sel, r.astype(F32), rank)
        work = jnp.where(sel, -jnp.inf, work)
        vals = jnp.where(slot == r, m, vals)
        return work, rank, vals

    _, rank, vals = lax.fori_loop(0, k, body, (s, jnp.full((n, tl), float(k), F32), jnp.zeros((k, tl), F32)))
    return rank, vals


def _select_kernel(s_ref, rank2_ref, cnt1_ref, p1_ref, p2_ref, *, heads, topk):
    def per_head(h, carry):
        s1 = s_ref[2 * h]
        s2 = s_ref[2 * h + 1]
        rank1, v1 = _top_ranks(s1, topk)
        rank2, v2 = _top_ranks(s2, topk)
        cand = jnp.concatenate([v1[r:r + 1] + v2 for r in range(topk)], axis=0)
        rank_c, cs = _top_ranks(cand, topk)
        taken = (rank_c < float(topk)).astype(F32)
        z = jnp.sum(jnp.exp(cs - cs[0:1]), axis=0, keepdims=True)
        cnt1 = jnp.zeros_like(s1)
        for r in range(topk):
            n_r = jnp.sum(taken[r * topk:(r + 1) * topk], axis=0, keepdims=True)
            cnt1 = cnt1 + jnp.where(rank1 == float(r), n_r, 0.0)
        rank2_ref[h] = rank2
        cnt1_ref[h] = cnt1
        p1_ref[h] = jnp.exp(s1 - v1[0:1]) / z
        p2_ref[h] = jnp.exp(s2 - v2[0:1])
        return carry

    lax.fori_loop(0, heads, per_head, 0)


def _peer_select(scores, heads):
    _, nkeys, t = scores.shape
    tl = _tile(t, LANES, LANES)
    shape = jax.ShapeDtypeStruct((heads, nkeys, t), F32)
    spec = pl.BlockSpec((heads, nkeys, tl), lambda i: (0, 0, i))
    return pl.pallas_call(
        functools.partial(_select_kernel, heads=heads, topk=PEER_TOPK),
        grid=(t // tl,),
        in_specs=[pl.BlockSpec((2 * heads, nkeys, tl), lambda i: (0, 0, i))],
        out_specs=[spec] * 4,
        out_shape=[shape] * 4,
        compiler_params=_params("parallel"),
        name="peer_select",
    )(scores)


def _peer_kernel(h2t_ref, u_ref, vt_ref, rank2_ref, cnt1_ref, p1_ref, p2_ref, x2_ref, fg_ref, y_ref, acc_ref,
                 *, heads, nkeys, n1):
    e = pl.program_id(1)

    @pl.when(e == 0)
    def _():
        acc_ref[...] = jnp.zeros_like(acc_ref)

    act = jax.nn.gelu(_dot(u_ref[...], h2t_ref[...]))
    gates = []
    for j in range(n1):
        i1 = e * n1 + j
        g = None
        for h in range(heads):
            c1 = cnt1_ref[h, pl.ds(i1, 1), :]
            w1 = p1_ref[h, pl.ds(i1, 1), :]
            term = jnp.where(rank2_ref[h] < c1, p2_ref[h], 0.0) * w1
            g = term if g is None else g + term
        gates.append(g)
    gate = gates[0] if n1 == 1 else jnp.concatenate(gates, axis=0)
    acc_ref[...] += _dot(vt_ref[...], (gate * act).astype(BF16))

    @pl.when(e == pl.num_programs(1) - 1)
    def _():
        y = x2_ref[...] + acc_ref[...].T
        y_ref[...] = y * lax.rsqrt(jnp.mean(y * y, axis=-1, keepdims=True) + EPS) * fg_ref[...]


def _peer_dense(h2t, u, vt, rank2, cnt1, p1, p2, x2, fg):
    d, t = h2t.shape
    n_exp = u.shape[0]
    heads, nkeys, _ = rank2.shape
    tb = _tile(t, 512, LANES)
    n1 = max(1, min(512, n_exp) // nkeys)
    eb = n1 * nkeys
    sel = pl.BlockSpec((heads, nkeys, tb), lambda i, e: (0, 0, i))
    return pl.pallas_call(
        functools.partial(_peer_kernel, heads=heads, nkeys=nkeys, n1=n1),
        grid=(t // tb, n_exp // eb),
        in_specs=[pl.BlockSpec((d, tb), lambda i, e: (0, i)),
                  pl.BlockSpec((eb, d), lambda i, e: (e, 0)),
                  pl.BlockSpec((d, eb), lambda i, e: (0, e)),
                  sel, sel, sel, sel,
                  pl.BlockSpec((tb, d), lambda i, e: (i, 0)),
                  pl.BlockSpec((1, d), lambda i, e: (0, 0))],
        out_specs=pl.BlockSpec((tb, d), lambda i, e: (i, 0)),
        out_shape=jax.ShapeDtypeStruct((t, d), F32),
        scratch_shapes=[pltpu.VMEM((d, tb), F32)],
        compiler_params=_params("parallel", "arbitrary"),
        name="peer_dense",
    )(h2t, u, vt, rank2, cnt1, p1, p2, x2, fg.reshape(1, d))


def kernel(x_prompt, x_sample, state_conv, state_gla, norm1_g, w_in, w_a2, b_a, w_dw, b_dw, conv_ln_g, conv_ln_b, w_conv_out, gla_norm_g, w_gla_o, w_mix_out, norm2_g, peer_wq, peer_k1, peer_k2, peer_u, peer_v, final_g):
    assert w_in.shape[0] == 1, "one layer"
    bp, lp, d = x_prompt.shape
    bs, ls, _ = x_sample.shape
    tp, ts = bp * lp, bs * ls
    cc = state_conv.shape[-1]
    width = w_dw.shape[1]
    _, _, heads, dk, dv = state_gla.shape
    gk, gv = heads * dk, heads * dv
    rank = w_a2.shape[1]
    nkeys, half = peer_k1.shape[1:]
    pheads = peer_wq.shape[2] // (2 * half)

    x = jnp.concatenate([x_prompt.reshape(tp, d), x_sample.reshape(ts, d)], axis=0)

    w = w_in[0].astype(BF16)
    o = 0
    w_ca, o = w[:, o:o + cc], o + cc
    w_cg, o = w[:, o:o + cc], o + cc
    w_qkv, o = w[:, o:o + 2 * gk + gv], o + 2 * gk + gv
    w_r, o = w[:, o:o + gv], o + gv
    w_a, o = w[:, o:o + rank], o + rank
    w_gab = w[:, o:o + 2 * d]
    rp = -(-rank // LANES) * LANES
    w_a = jnp.pad(w_a, ((0, 0), (0, rp - rank)))
    w_a2p = jnp.pad(w_a2[0], ((0, rp - rank), (0, 0)))

    h = _rmsnorm(x, norm1_g[0], BF16)
    u = _glu_proj(h, w_ca, w_cg)
    zqkv = _linear(h, w_qkv)
    zrgg = _act_linear(h, jnp.concatenate([w_r, w_gab], axis=1), gv)
    lg = _decay_proj(h, w_a, w_a2p, b_a[0])

    u_p, u_s = u[:tp].reshape(bp, lp, cc), u[tp:].reshape(bs, ls, cc)
    lt_p = _tile(lp, 256)
    ext_p, full_p = _conv_windows(u_p, jnp.zeros((bp, width - 1, cc), F32), lt_p)
    ext_s, full_s = _conv_windows(u_s, state_conv[0], ls)
    conv = lambda ext, lt, nb: _conv_branch(ext, w_dw[0], b_dw[0], conv_ln_g[0], conv_ln_b[0], lt, nb)
    cact = jnp.concatenate([conv(ext_p, lt_p, 1).reshape(tp, cc),
                            conv(ext_s, ls, _tile(bs, 16, 1)).reshape(ts, cc)], axis=0)
    conv_p = full_p[:, full_p.shape[1] - (width - 1):]
    conv_s = full_s[:, full_s.shape[1] - (width - 1):]

    gla = functools.partial(_gla_branch, zqkv, lg, zrgg, gn=gla_norm_g[0])
    og_p, gla_p = gla(jnp.zeros((bp, heads, dk, dv), F32), row0=0, seq_len=lp, nb=1)
    og_s, gla_s = gla(state_gla[0], row0=tp, seq_len=ls, nb=_tile(bs, 4, 2) if ls < GLA_CHUNK else 1)
    og = jnp.concatenate([og_p, og_s], axis=0)

    merged = _merge(cact, og, w_conv_out[0].astype(BF16), w_gla_o[0].astype(BF16), zrgg, gv, gv + d)
    x2, h2 = _mix(merged, w_mix_out[0].astype(BF16), x, norm2_g[0])

    h2t = h2.T
    scores = _peer_scores(h2t, peer_wq[0].T.astype(BF16), peer_k1[0].astype(BF16), peer_k2[0].astype(BF16), pheads)
    rank2, cnt1, p1, p2 = _peer_select(scores, pheads)
    y = _peer_dense(h2t, peer_u[0].astype(BF16), peer_v[0].T.astype(BF16), rank2, cnt1, p1, p2, x2, final_g)

    return (y[:tp].reshape(bp, lp, d), y[tp:].reshape(bs, ls, d),
            conv_p[None].astype(state_conv.dtype), gla_p[None].astype(state_gla.dtype),
            conv_s[None].astype(state_conv.dtype), gla_s[None].astype(state_gla.dtype))
```

```python
import functools
import math

import numpy as np
import jax
import jax.numpy as jnp
from jax import lax
from jax.experimental import pallas as pl
from jax.experimental.pallas import tpu as pltpu

EPS = 1e-6
GLA_CHUNK = 64
GATE_TEMP = 16.0
PEER_TOPK = 16
CONV_HIST_ROWS = 32
LANES = 128
VMEM_LIMIT_BYTES = 56 * 1024 * 1024

BF16 = jnp.bfloat16
F32 = jnp.float32


def _tile(n, pref, mult=8):
    if n <= pref:
        return n
    for t in range(pref - pref % mult, 0, -mult):
        if n % t == 0:
            return t
    return n


def _params(*sem):
    return pltpu.CompilerParams(dimension_semantics=sem, vmem_limit_bytes=VMEM_LIMIT_BYTES)


def _dot(a, b):
    return jnp.dot(a, b, preferred_element_type=F32)


def _dot_tn(a, b):
    return lax.dot_general(a, b, (((0,), (0,)), ((), ())), preferred_element_type=F32)


def _dot_nt(a, b):
    return lax.dot_general(a, b, (((1,), (1,)), ((), ())), preferred_element_type=F32)


def _split3(x):
    hi = x.astype(BF16)
    r1 = x - hi.astype(F32)
    mid = r1.astype(BF16)
    lo = (r1 - mid.astype(F32)).astype(BF16)
    return hi, mid, lo


def _rmsnorm_kernel(x_ref, g_ref, o_ref):
    x = x_ref[...]
    y = x * lax.rsqrt(jnp.mean(x * x, axis=-1, keepdims=True) + EPS)
    o_ref[...] = (y * g_ref[...]).astype(o_ref.dtype)


def _rmsnorm(x, g, out_dtype):
    t, d = x.shape
    tm = _tile(t, 512)
    return pl.pallas_call(
        _rmsnorm_kernel,
        grid=(t // tm,),
        in_specs=[pl.BlockSpec((tm, d), lambda i: (i, 0)), pl.BlockSpec((1, d), lambda i: (0, 0))],
        out_specs=pl.BlockSpec((tm, d), lambda i: (i, 0)),
        out_shape=jax.ShapeDtypeStruct((t, d), out_dtype),
        compiler_params=_params("parallel"),
        name="rmsnorm",
    )(x, g.reshape(1, d))


def _glu_kernel(h_ref, wa_ref, wg_ref, o_ref):
    h = h_ref[...]
    o_ref[...] = _dot(h, wa_ref[...]) * jax.nn.sigmoid(_dot(h, wg_ref[...]))


def _glu_proj(h, wa, wg):
    t, k = h.shape
    n = wa.shape[1]
    tm, tn = _tile(t, 1024), _tile(n, 512, LANES)
    return pl.pallas_call(
        _glu_kernel,
        grid=(t // tm, n // tn),
        in_specs=[pl.BlockSpec((tm, k), lambda i, j: (i, 0)),
                  pl.BlockSpec((k, tn), lambda i, j: (0, j)),
                  pl.BlockSpec((k, tn), lambda i, j: (0, j))],
        out_specs=pl.BlockSpec((tm, tn), lambda i, j: (i, j)),
        out_shape=jax.ShapeDtypeStruct((t, n), F32),
        compiler_params=_params("parallel", "parallel"),
        name="glu_proj",
    )(h, wa, wg)


def _linear_kernel(h_ref, w_ref, o_ref):
    o_ref[...] = _dot(h_ref[...], w_ref[...])


def _linear(h, w):
    t, k = h.shape
    n = w.shape[1]
    tm, tn = _tile(t, 1024), _tile(n, 1024, LANES)
    return pl.pallas_call(
        _linear_kernel,
        grid=(t // tm, n // tn),
        in_specs=[pl.BlockSpec((tm, k), lambda i, j: (i, 0)), pl.BlockSpec((k, tn), lambda i, j: (0, j))],
        out_specs=pl.BlockSpec((tm, tn), lambda i, j: (i, j)),
        out_shape=jax.ShapeDtypeStruct((t, n), F32),
        compiler_params=_params("parallel", "parallel"),
        name="qkv_proj",
    )(h, w)


def _act_linear_kernel(h_ref, w_ref, o_ref, *, n_silu_tiles):
    z = _dot(h_ref[...], w_ref[...])
    sig = jax.nn.sigmoid(z)
    o_ref[...] = jnp.where(pl.program_id(1) < n_silu_tiles, z * sig, sig).astype(o_ref.dtype)


def _act_linear(h, w, n_silu):
    t, k = h.shape
    n = w.shape[1]
    tm, tn = _tile(t, 1024), _tile(math.gcd(n, n_silu), 1024, LANES)
    return pl.pallas_call(
        functools.partial(_act_linear_kernel, n_silu_tiles=n_silu // tn),
        grid=(t // tm, n // tn),
        in_specs=[pl.BlockSpec((tm, k), lambda i, j: (i, 0)), pl.BlockSpec((k, tn), lambda i, j: (0, j))],
        out_specs=pl.BlockSpec((tm, tn), lambda i, j: (i, j)),
        out_shape=jax.ShapeDtypeStruct((t, n), BF16),
        compiler_params=_params("parallel", "parallel"),
        name="gate_proj",
    )(h, w)


def _decay_kernel(h_ref, wa_ref, wa2_ref, ba_ref, o_ref):
    za = _dot(h_ref[...], wa_ref[...])
    x = jnp.dot(za, wa2_ref[...], preferred_element_type=F32, precision=lax.Precision.HIGHEST) + ba_ref[...]
    o_ref[...] = (jnp.minimum(x, 0.0) - jnp.log1p(jnp.exp(-jnp.abs(x)))) * (1.0 / GATE_TEMP)


def _decay_proj(h, wa, wa2, ba):
    t, k = h.shape
    rp = wa.shape[1]
    n = wa2.shape[1]
    tm = _tile(t, 512)
    return pl.pallas_call(
        _decay_kernel,
        grid=(t // tm,),
        in_specs=[pl.BlockSpec((tm, k), lambda i: (i, 0)),
                  pl.BlockSpec((k, rp), lambda i: (0, 0)),
                  pl.BlockSpec((rp, n), lambda i: (0, 0)),
                  pl.BlockSpec((1, n), lambda i: (0, 0))],
        out_specs=pl.BlockSpec((tm, n), lambda i: (i, 0)),
        out_shape=jax.ShapeDtypeStruct((t, n), F32),
        compiler_params=_params("parallel"),
        name="decay_proj",
    )(h, wa, wa2, ba.reshape(1, n))


def _conv_kernel(ext_ref, w_ref, b_ref, g_ref, beta_ref, o_ref, *, nb, lt, width, rt):
    off = CONV_HIST_ROWS - (width - 1)
    bias, gain, beta = b_ref[...], g_ref[...], beta_ref[...]

    def row_tile(n, r0):
        acc = ext_ref[n, pl.ds(r0 + off, rt), :] * w_ref[0:1, :]
        for j in range(1, width):
            acc = acc + ext_ref[n, pl.ds(r0 + off + j, rt), :] * w_ref[j:j + 1, :]
        c = acc + bias
        mu = jnp.mean(c, axis=-1, keepdims=True)
        d = c - mu
        var = jnp.mean(d * d, axis=-1, keepdims=True)
        y = d * lax.rsqrt(var + EPS) * gain + beta
        o_ref[n, pl.ds(r0, rt), :] = y * jax.nn.sigmoid(y)

    for n in range(nb):
        for r0 in range(0, lt, rt):
            row_tile(n, r0)


def _conv_branch(ext, w_dw, b_dw, ln_g, ln_b, lt, nb):
    n, rows, c = ext.shape
    width = w_dw.shape[0]
    rt = _tile(lt, 32)
    row = lambda a: a.reshape(1, c)
    return pl.pallas_call(
        functools.partial(_conv_kernel, nb=nb, lt=lt, width=width, rt=rt),
        grid=(n // nb,),
        in_specs=[pl.BlockSpec((nb, rows, c), lambda i: (i, 0, 0)),
                  pl.BlockSpec((width, c), lambda i: (0, 0)),
                  pl.BlockSpec((1, c), lambda i: (0, 0)),
                  pl.BlockSpec((1, c), lambda i: (0, 0)),
                  pl.BlockSpec((1, c), lambda i: (0, 0))],
        out_specs=pl.BlockSpec((nb, lt, c), lambda i: (i, 0, 0)),
        out_shape=jax.ShapeDtypeStruct((n, lt, c), F32),
        compiler_params=_params("parallel"),
        name="conv_branch",
    )(ext, w_dw, row(b_dw), row(ln_g), row(ln_b))


def _conv_windows(u_seq, hist, lt):
    n, l, c = u_seq.shape
    front = jnp.zeros((n, CONV_HIST_ROWS - hist.shape[1], c), u_seq.dtype)
    full = jnp.concatenate([front, hist, u_seq], axis=1)
    wins = [full[:, i * lt:i * lt + CONV_HIST_ROWS + lt] for i in range(l // lt)]
    return jnp.stack(wins, axis=1).reshape(n * (l // lt), CONV_HIST_ROWS + lt, c), full


def _gla_tables(c):
    nlev = int(math.log2(c))
    assert 2 ** nlev == c
    t = np.arange(c)
    tri = (t[None, :] <= t[:, None]).astype(np.float32)
    mats = [tri]
    masks = []
    for l in range(nlev):
        hs = 2 ** l
        mid = (t // (2 * hs)) * (2 * hs) + hs
        mats.append(tri[mid - 1])
        same = (t[:, None] // (2 * hs)) == (t[None, :] // (2 * hs))
        upper = (t % (2 * hs)) >= hs
        masks.append((same & upper[:, None] & ~upper[None, :]).astype(np.float32))
    mats.append(np.ones((c, c), np.float32))
    return jnp.asarray(np.concatenate(mats, axis=0), BF16), jnp.asarray(np.stack(masks)), nlev


def _gla_kernel(q_ref, k_ref, v_ref, lg_ref, r_ref, s0_ref, gn_ref, mall_ref, masks_ref, o_ref, sout_ref, s_scr,
                *, nb, c, nlev, scale):
    chunk = pl.program_id(2)

    @pl.when(chunk == 0)
    def _():
        s_scr[...] = s0_ref[:, 0]

    dk = q_ref.shape[1]
    dv = v_ref.shape[1]
    mall = mall_ref[...]
    gate = r_ref[...].astype(F32)
    row_id = lax.broadcasted_iota(jnp.int32, (c, dk), 0)
    eye = lax.broadcasted_iota(jnp.int32, (c, c), 0) == lax.broadcasted_iota(jnp.int32, (c, c), 1)
    ones = jnp.ones((c, LANES), BF16)
    outs = []
    for n in range(nb):
        rows = slice(n * c, (n + 1) * c)
        q = q_ref[rows, :] * scale
        k = k_ref[rows, :]
        v = v_ref[rows, :].astype(BF16)
        parts = _split3(lg_ref[rows, :])
        ball = _dot(mall, parts[0]) + _dot(mall, parts[1]) + _dot(mall, parts[2])
        b = ball[0:c]
        b_last = ball[(nlev + 1) * c:(nlev + 2) * c]
        att = jnp.where(eye, jnp.sum(q * k, axis=-1, keepdims=True), 0.0)
        for l in range(nlev):
            ref = ball[(l + 1) * c:(l + 2) * c]
            upper = (row_id & (2 ** (l + 1) - 1)) >= 2 ** l
            x = (jnp.where(upper, q, k) * jnp.exp(-jnp.abs(b - ref))).astype(BF16)
            att = att + _dot_nt(x, x) * masks_ref[l]
        s = s_scr[n]
        o = _dot(att.astype(BF16), v) + _dot((q * jnp.exp(b)).astype(BF16), s.astype(BF16))
        kd = (k * jnp.exp(b_last - b)).astype(BF16)
        b_col = _dot_tn(parts[0], ones) + _dot_tn(parts[1], ones) + _dot_tn(parts[2], ones)
        decay = jnp.concatenate([jnp.exp(b_col)] * (dv // LANES), axis=1)
        s_scr[n] = decay * s + _dot_tn(kd, v)
        on = o * lax.rsqrt(jnp.mean(o * o, axis=-1, keepdims=True) + EPS)
        outs.append(on * gn_ref[...] * gate[rows, :])
    o_ref[...] = (outs[0] if nb == 1 else jnp.concatenate(outs, axis=0)).astype(o_ref.dtype)

    @pl.when(chunk == pl.num_programs(2) - 1)
    def _():
        sout_ref[:, 0] = s_scr[...]


def _gla_branch(zqkv, lg, zrgg, s0, gn, *, row0, seq_len, nb):
    nseq, heads, dk, dv = s0.shape
    c = min(GLA_CHUNK, seq_len)
    nc = seq_len // c
    assert seq_len % c == 0 and (nb == 1 or nc == 1) and nseq % nb == 0
    rows = nb * c
    assert row0 % rows == 0 and (2 * heads * dk) % dv == 0 and dv % LANES == 0
    base = row0 // rows
    voff = 2 * heads * dk // dv
    mall, masks, nlev = _gla_tables(c)
    rmap = lambda s, h, ch: base + s * nc + ch
    t_out = nseq * seq_len
    return pl.pallas_call(
        functools.partial(_gla_kernel, nb=nb, c=c, nlev=nlev, scale=dk ** -0.5),
        grid=(nseq // nb, heads, nc),
        in_specs=[pl.BlockSpec((rows, dk), lambda s, h, ch: (rmap(s, h, ch), h)),
                  pl.BlockSpec((rows, dk), lambda s, h, ch: (rmap(s, h, ch), heads + h)),
                  pl.BlockSpec((rows, dv), lambda s, h, ch: (rmap(s, h, ch), voff + h)),
                  pl.BlockSpec((rows, dk), lambda s, h, ch: (rmap(s, h, ch), h)),
                  pl.BlockSpec((rows, dv), lambda s, h, ch: (rmap(s, h, ch), h)),
                  pl.BlockSpec((nb, 1, dk, dv), lambda s, h, ch: (s, h, 0, 0)),
                  pl.BlockSpec((1, dv), lambda s, h, ch: (0, h)),
                  pl.BlockSpec(mall.shape, lambda s, h, ch: (0, 0)),
                  pl.BlockSpec(masks.shape, lambda s, h, ch: (0, 0, 0))],
        out_specs=[pl.BlockSpec((rows, dv), lambda s, h, ch: (s * nc + ch, h)),
                   pl.BlockSpec((nb, 1, dk, dv), lambda s, h, ch: (s, h, 0, 0))],
        out_shape=[jax.ShapeDtypeStruct((t_out, heads * dv), BF16),
                   jax.ShapeDtypeStruct((nseq, heads, dk, dv), F32)],
        scratch_shapes=[pltpu.VMEM((nb, dk, dv), F32)],
        compiler_params=_params("parallel", "parallel", "arbitrary"),
        name="gla_branch",
    )(zqkv, zqkv, zqkv, lg, zrgg, s0, gn.reshape(1, heads * dv), mall, masks)


def _merge_kernel(c_ref, og_ref, wc_ref, wg_ref, ga_ref, gb_ref, o_ref):
    a = _dot(c_ref[...].astype(BF16), wc_ref[...])
    b = _dot(og_ref[...], wg_ref[...])
    o_ref[...] = (ga_ref[...].astype(F32) * a + gb_ref[...].astype(F32) * b).astype(o_ref.dtype)


def _merge(cact, og, wc, wg, zrgg, ga_col0, gb_col0):
    t, kc = cact.shape
    kg = og.shape[1]
    n = wc.shape[1]
    tm, tn = _tile(t, 512), _tile(math.gcd(math.gcd(n, ga_col0), gb_col0), 1024, LANES)
    return pl.pallas_call(
        _merge_kernel,
        grid=(t // tm, n // tn),
        in_specs=[pl.BlockSpec((tm, kc), lambda i, j: (i, 0)),
                  pl.BlockSpec((tm, kg), lambda i, j: (i, 0)),
                  pl.BlockSpec((kc, tn), lambda i, j: (0, j)),
                  pl.BlockSpec((kg, tn), lambda i, j: (0, j)),
                  pl.BlockSpec((tm, tn), lambda i, j: (i, ga_col0 // tn + j)),
                  pl.BlockSpec((tm, tn), lambda i, j: (i, gb_col0 // tn + j))],
        out_specs=pl.BlockSpec((tm, tn), lambda i, j: (i, j)),
        out_shape=jax.ShapeDtypeStruct((t, n), BF16),
        compiler_params=_params("parallel", "parallel"),
        name="merge_proj",
    )(cact, og, wc, wg, zrgg, zrgg)


def _mix_kernel(m_ref, w_ref, x_ref, g_ref, x2_ref, h2_ref):
    x2 = x_ref[...] + _dot(m_ref[...], w_ref[...])
    x2_ref[...] = x2
    y = x2 * lax.rsqrt(jnp.mean(x2 * x2, axis=-1, keepdims=True) + EPS)
    h2_ref[...] = (y * g_ref[...]).astype(h2_ref.dtype)


def _mix(merged, w, x, g):
    t, d = x.shape
    k = merged.shape[1]
    tm = _tile(t, 256)
    return pl.pallas_call(
        _mix_kernel,
        grid=(t // tm,),
        in_specs=[pl.BlockSpec((tm, k), lambda i: (i, 0)),
                  pl.BlockSpec((k, d), lambda i: (0, 0)),
                  pl.BlockSpec((tm, d), lambda i: (i, 0)),
                  pl.BlockSpec((1, d), lambda i: (0, 0))],
        out_specs=[pl.BlockSpec((tm, d), lambda i: (i, 0)), pl.BlockSpec((tm, d), lambda i: (i, 0))],
        out_shape=[jax.ShapeDtypeStruct((t, d), F32), jax.ShapeDtypeStruct((t, d), BF16)],
        compiler_params=_params("parallel"),
        name="mix_proj",
    )(merged, w, x, g.reshape(1, d))


def _score_kernel(h2t_ref, wqt_ref, k1_ref, k2_ref, s_ref, *, heads, half):
    qt = _dot(wqt_ref[...], h2t_ref[...])
    k1, k2 = k1_ref[...], k2_ref[...]
    for h in range(heads):
        lo = 2 * half * h
        s_ref[2 * h] = _dot(k1, qt[lo:lo + half].astype(BF16))
        s_ref[2 * h + 1] = _dot(k2, qt[lo + half:lo + 2 * half].astype(BF16))


def _peer_scores(h2t, wqt, k1, k2, heads):
    d, t = h2t.shape
    nq = wqt.shape[0]
    nkeys, half = k1.shape
    tl = _tile(t, 512, LANES)
    return pl.pallas_call(
        functools.partial(_score_kernel, heads=heads, half=half),
        grid=(t // tl,),
        in_specs=[pl.BlockSpec((d, tl), lambda i: (0, i)),
                  pl.BlockSpec((nq, d), lambda i: (0, 0)),
                  pl.BlockSpec((nkeys, half), lambda i: (0, 0)),
                  pl.BlockSpec((nkeys, half), lambda i: (0, 0))],
        out_specs=pl.BlockSpec((2 * heads, nkeys, tl), lambda i: (0, 0, i)),
        out_shape=jax.ShapeDtypeStruct((2 * heads, nkeys, t), F32),
        compiler_params=_params("parallel"),
        name="peer_scores",
    )(h2t, wqt, k1, k2)


def _top_ranks(s, k):
    n, tl = s.shape
    idx = lax.broadcasted_iota(jnp.int32, (n, tl), 0).astype(F32)
    slot = lax.broadcasted_iota(jnp.int32, (k, tl), 0)

    def body(r, carry):
        work, rank, vals = carry
        m = jnp.max(work, axis=0, keepdims=True)
        first = jnp.min(jnp.where(work == m, idx, float(n)), axis=0, keepdims=True)
        sel = idx == first
        rank = jnp.where(sel, r.astype(F32), rank)
        work = jnp.where(sel, -jnp.inf, work)
        vals = jnp.where(slot == r, m, vals)
        return work, rank, vals

    _, rank, vals = lax.fori_loop(0, k, body, (s, jnp.full((n, tl), float(k), F32), jnp.zeros((k, tl), F32)))
    return rank, vals


def _select_kernel(s_ref, rank2_ref, cnt1_ref, p1_ref, p2_ref, *, heads, topk):
    def per_head(h, carry):
        s1 = s_ref[2 * h]
        s2 = s_ref[2 * h + 1]
        rank1, v1 = _top_ranks(s1, topk)
        rank2, v2 = _top_ranks(s2, topk)
        cand = jnp.concatenate([v1[r:r + 1] + v2 for r in range(topk)], axis=0)
        rank_c, cs = _top_ranks(cand, topk)
        taken = (rank_c < float(topk)).astype(F32)
        z = jnp.sum(jnp.exp(cs - cs[0:1]), axis=0, keepdims=True)
        cnt1 = jnp.zeros_like(s1)
        for r in range(topk):
            n_r = jnp.sum(taken[r * topk:(r + 1) * topk], axis=0, keepdims=True)
            cnt1 = cnt1 + jnp.where(rank1 == float(r), n_r, 0.0)
        rank2_ref[h, 0] = rank2.astype(rank2_ref.dtype)
        cnt1_ref[h, 0] = cnt1
        p1_ref[h, 0] = jnp.exp(s1 - v1[0:1]) / z
        p2_ref[h, 0] = jnp.exp(s2 - v2[0:1]).astype(p2_ref.dtype)
        return carry

    lax.fori_loop(0, heads, per_head, 0)


def _peer_select(scores, heads):
    _, nkeys, t = scores.shape
    assert t % LANES == 0
    shape = lambda dt: jax.ShapeDtypeStruct((heads, t // LANES, nkeys, LANES), dt)
    spec = pl.BlockSpec((heads, 1, nkeys, LANES), lambda i: (0, i, 0, 0))
    return pl.pallas_call(
        functools.partial(_select_kernel, heads=heads, topk=PEER_TOPK),
        grid=(t // LANES,),
        in_specs=[pl.BlockSpec((2 * heads, nkeys, LANES), lambda i: (0, 0, i))],
        out_specs=[spec] * 4,
        out_shape=[shape(F32)] * 4,
        compiler_params=_params("parallel"),
        name="peer_select",
    )(scores)


def _peer_kernel(h2t_ref, u_ref, vt_ref, rank2_ref, cnt1_ref, p1_ref, p2_ref, x2_ref, fg_ref, y_ref, acc_ref,
                 act0_ref, act1_ref, w0_ref, w1_ref, *, heads, nkeys, n1):
    s = pl.program_id(1)
    last = pl.num_programs(1) - 1
    ne = last - 1

    @pl.when(s == 0)
    def _():
        acc_ref[...] = jnp.zeros_like(acc_ref)
        act1_ref[...] = jnp.zeros_like(act1_ref)
        w0_ref[...] = jnp.zeros_like(w0_ref)

    def step(act_prev, w_next, act_next, w_prev):
        blk = jnp.clip(s - 1, 0, ne - 1)
        nq = act_prev.shape[1] // LANES
        d, dm = acc_ref.shape[0], u_ref.shape[1]

        def gate_piece(q, j):
            rows, lanes = slice(j * nkeys, (j + 1) * nkeys), slice(q * LANES, (q + 1) * LANES)
            i1 = blk * n1 + j
            g = None
            for h in range(heads):
                c1 = cnt1_ref[h, q, pl.ds(i1, 1), :]
                w1 = p1_ref[h, q, pl.ds(i1, 1), :]
                term = jnp.where(rank2_ref[h, q] < c1, p2_ref[h, q], 0.0) * w1
                g = term if g is None else g + term
            w_next[rows, lanes] = (g * jax.nn.gelu(act_prev[rows, lanes])).astype(BF16)

        def act_piece(k):
            ks = slice(k * dm // nq, (k + 1) * dm // nq)
            part = _dot(u_ref[:, ks], h2t_ref[ks, :])
            if k == 0:
                act_next[...] = part
            else:
                act_next[...] += part

        def out_piece(r):
            rs = slice(r * d // nq, (r + 1) * d // nq)
            acc_ref[rs, :] += _dot(vt_ref[rs, :], w_prev[...])

        for q in range(nq):
            for j in range(n1):
                gate_piece(q, j)
                if j == n1 // 2 - 1 or n1 == 1:
                    act_piece(q)
            out_piece(q)

    @pl.when(s % 2 == 0)
    def _():
        step(act1_ref, w1_ref, act0_ref, w0_ref)

    @pl.when(s % 2 == 1)
    def _():
        step(act0_ref, w0_ref, act1_ref, w1_ref)

    @pl.when(s == last)
    def _():
        y = x2_ref[...] + acc_ref[...].T
        y_ref[...] = y * lax.rsqrt(jnp.mean(y * y, axis=-1, keepdims=True) + EPS) * fg_ref[...]


def _peer_dense(h2t, u, vt, rank2, cnt1, p1, p2, x2, fg):
    d, t = h2t.shape
    n_exp = u.shape[0]
    heads, _, nkeys, _ = rank2.shape
    tb = _tile(t, 512, LANES)
    assert tb % LANES == 0
    n1 = max(1, min(512, n_exp) // nkeys)
    eb = n1 * nkeys
    ne = n_exp // eb
    sel = pl.BlockSpec((heads, tb // LANES, nkeys, LANES), lambda i, e: (0, i, 0, 0))
    return pl.pallas_call(
        functools.partial(_peer_kernel, heads=heads, nkeys=nkeys, n1=n1),
        grid=(t // tb, ne + 2),
        in_specs=[pl.BlockSpec((d, tb), lambda i, e: (0, i)),
                  pl.BlockSpec((eb, d), lambda i, e: (jnp.minimum(e, ne - 1), 0)),
                  pl.BlockSpec((d, eb), lambda i, e: (0, jnp.clip(e - 2, 0, ne - 1))),
                  sel, sel, sel, sel,
                  pl.BlockSpec((tb, d), lambda i, e: (i, 0)),
                  pl.BlockSpec((1, d), lambda i, e: (0, 0))],
        out_specs=pl.BlockSpec((tb, d), lambda i, e: (i, 0)),
        out_shape=jax.ShapeDtypeStruct((t, d), F32),
        scratch_shapes=[pltpu.VMEM((d, tb), F32), pltpu.VMEM((eb, tb), F32), pltpu.VMEM((eb, tb), F32),
                        pltpu.VMEM((eb, tb), BF16), pltpu.VMEM((eb, tb), BF16)],
        compiler_params=_params("parallel", "arbitrary"),
        name="peer_dense",
    )(h2t, u, vt, rank2, cnt1, p1, p2, x2, fg.reshape(1, d))


def kernel(x_prompt, x_sample, state_conv, state_gla, norm1_g, w_in, w_a2, b_a, w_dw, b_dw, conv_ln_g, conv_ln_b, w_conv_out, gla_norm_g, w_gla_o, w_mix_out, norm2_g, peer_wq, peer_k1, peer_k2, peer_u, peer_v, final_g):
    assert w_in.shape[0] == 1, "one layer"
    bp, lp, d = x_prompt.shape
    bs, ls, _ = x_sample.shape
    tp, ts = bp * lp, bs * ls
    cc = state_conv.shape[-1]
    width = w_dw.shape[1]
    _, _, heads, dk, dv = state_gla.shape
    gk, gv = heads * dk, heads * dv
    rank = w_a2.shape[1]
    nkeys, half = peer_k1.shape[1:]
    pheads = peer_wq.shape[2] // (2 * half)

    x = jnp.concatenate([x_prompt.reshape(tp, d), x_sample.reshape(ts, d)], axis=0)

    w = w_in[0].astype(BF16)
    o = 0
    w_ca, o = w[:, o:o + cc], o + cc
    w_cg, o = w[:, o:o + cc], o + cc
    w_qkv, o = w[:, o:o + 2 * gk + gv], o + 2 * gk + gv
    w_r, o = w[:, o:o + gv], o + gv
    w_a, o = w[:, o:o + rank], o + rank
    w_gab = w[:, o:o + 2 * d]
    rp = -(-rank // LANES) * LANES
    w_a = jnp.pad(w_a, ((0, 0), (0, rp - rank)))
    w_a2p = jnp.pad(w_a2[0], ((0, rp - rank), (0, 0)))

    h = _rmsnorm(x, norm1_g[0], BF16)
    u = _glu_proj(h, w_ca, w_cg)
    zqkv = _linear(h, w_qkv)
    zrgg = _act_linear(h, jnp.concatenate([w_r, w_gab], axis=1), gv)
    lg = _decay_proj(h, w_a, w_a2p, b_a[0])

    u_p, u_s = u[:tp].reshape(bp, lp, cc), u[tp:].reshape(bs, ls, cc)
    lt_p = _tile(lp, 256)
    ext_p, full_p = _conv_windows(u_p, jnp.zeros((bp, width - 1, cc), F32), lt_p)
    ext_s, full_s = _conv_windows(u_s, state_conv[0], ls)
    conv = lambda ext, lt, nb: _conv_branch(ext, w_dw[0], b_dw[0], conv_ln_g[0], conv_ln_b[0], lt, nb)
    cact = jnp.concatenate([conv(ext_p, lt_p, 1).reshape(tp, cc),
                            conv(ext_s, ls, _tile(bs, 16, 1)).reshape(ts, cc)], axis=0)
    conv_p = full_p[:, full_p.shape[1] - (width - 1):]
    conv_s = full_s[:, full_s.shape[1] - (width - 1):]

    gla = functools.partial(_gla_branch, zqkv, lg, zrgg, gn=gla_norm_g[0])
    og_p, gla_p = gla(jnp.zeros((bp, heads, dk, dv), F32), row0=0, seq_len=lp, nb=1)
    og_s, gla_s = gla(state_gla[0], row0=tp, seq_len=ls, nb=_tile(bs, 4, 2) if ls < GLA_CHUNK else 1)
    og = jnp.concatenate([og_p, og_s], axis=0)

    merged = _merge(cact, og, w_conv_out[0].astype(BF16), w_gla_o[0].astype(BF16), zrgg, gv, gv + d)
    x2, h2 = _mix(merged, w_mix_out[0].astype(BF16), x, norm2_g[0])

    h2t = h2.T
    scores = _peer_scores(h2t, peer_wq[0].T.astype(BF16), peer_k1[0].astype(BF16), peer_k2[0].astype(BF16), pheads)
    rank2, cnt1, p1, p2 = _peer_select(scores, pheads)
    y = _peer_dense(h2t, peer_u[0].astype(BF16), peer_v[0].T.astype(BF16), rank2, cnt1, p1, p2, x2, final_g)

    return (y[:tp].reshape(bp, lp, d), y[tp:].reshape(bs, ls, d),
            conv_p[None].astype(state_conv.dtype), gla_p[None].astype(state_gla.dtype),
            conv_s[None].astype(state_conv.dtype), gla_s[None].astype(state_gla.dtype))
```

```python
import functools
import math

import numpy as np
import jax
import jax.numpy as jnp
from jax import lax
from jax.experimental import pallas as pl
from jax.experimental.pallas import tpu as pltpu

EPS = 1e-6
GLA_CHUNK = 64
GATE_TEMP = 16.0
PEER_TOPK = 16
CONV_HIST_ROWS = 32
LANES = 128
VMEM_LIMIT_BYTES = 56 * 1024 * 1024

BF16 = jnp.bfloat16
F32 = jnp.float32


def _tile(n, pref, mult=8):
    if n <= pref:
        return n
    for t in range(pref - pref % mult, 0, -mult):
        if n % t == 0:
            return t
    return n


def _params(*sem):
    return pltpu.CompilerParams(dimension_semantics=sem, vmem_limit_bytes=VMEM_LIMIT_BYTES)


def _dot(a, b):
    return jnp.dot(a, b, preferred_element_type=F32)


def _dot_tn(a, b):
    return lax.dot_general(a, b, (((0,), (0,)), ((), ())), preferred_element_type=F32)


def _dot_nt(a, b):
    return lax.dot_general(a, b, (((1,), (1,)), ((), ())), preferred_element_type=F32)


def _split3(x):
    hi = x.astype(BF16)
    r1 = x - hi.astype(F32)
    mid = r1.astype(BF16)
    lo = (r1 - mid.astype(F32)).astype(BF16)
    return hi, mid, lo


def _two_part_specs(tm, cols, n_first, grid_rank=1):
    if grid_rank == 1:
        return (pl.BlockSpec((tm, cols), lambda i: (jnp.minimum(i, n_first - 1), 0)),
                pl.BlockSpec((tm, cols), lambda i: (jnp.maximum(i - n_first, 0), 0)))
    return (pl.BlockSpec((tm, cols), lambda i, j: (jnp.minimum(i, n_first - 1), 0)),
            pl.BlockSpec((tm, cols), lambda i, j: (jnp.maximum(i - n_first, 0), 0)))


def _pick_part(n_first, first_ref, second_ref, emit):
    i = pl.program_id(0)

    @pl.when(i < n_first)
    def _():
        emit(first_ref)

    @pl.when(i >= n_first)
    def _():
        emit(second_ref)


def _rmsnorm_kernel(xp_ref, xs_ref, g_ref, o_ref, *, n_first):
    def emit(x_ref):
        x = x_ref[...]
        y = x * lax.rsqrt(jnp.mean(x * x, axis=-1, keepdims=True) + EPS)
        o_ref[...] = (y * g_ref[...]).astype(o_ref.dtype)

    _pick_part(n_first, xp_ref, xs_ref, emit)


def _rmsnorm(xp, xs, g, out_dtype):
    (tp, d), ts = xp.shape, xs.shape[0]
    tm = _tile(math.gcd(tp, ts), 512)
    return pl.pallas_call(
        functools.partial(_rmsnorm_kernel, n_first=tp // tm),
        grid=((tp + ts) // tm,),
        in_specs=[*_two_part_specs(tm, d, tp // tm), pl.BlockSpec((1, d), lambda i: (0, 0))],
        out_specs=pl.BlockSpec((tm, d), lambda i: (i, 0)),
        out_shape=jax.ShapeDtypeStruct((tp + ts, d), out_dtype),
        compiler_params=_params("parallel"),
        name="rmsnorm",
    )(xp, xs, g.reshape(1, d))


def _glu_kernel(h_ref, wa_ref, wg_ref, o_ref):
    h = h_ref[...]
    o_ref[...] = _dot(h, wa_ref[...]) * jax.nn.sigmoid(_dot(h, wg_ref[...]))


def _glu_proj(h, wa, wg):
    t, k = h.shape
    n = wa.shape[1]
    tm, tn = _tile(t, 1024), _tile(n, 512, LANES)
    return pl.pallas_call(
        _glu_kernel,
        grid=(t // tm, n // tn),
        in_specs=[pl.BlockSpec((tm, k), lambda i, j: (i, 0)),
                  pl.BlockSpec((k, tn), lambda i, j: (0, j)),
                  pl.BlockSpec((k, tn), lambda i, j: (0, j))],
        out_specs=pl.BlockSpec((tm, tn), lambda i, j: (i, j)),
        out_shape=jax.ShapeDtypeStruct((t, n), F32),
        compiler_params=_params("parallel", "parallel"),
        name="glu_proj",
    )(h, wa, wg)


def _linear_kernel(h_ref, w_ref, o_ref):
    o_ref[...] = _dot(h_ref[...], w_ref[...])


def _linear(h, w):
    t, k = h.shape
    n = w.shape[1]
    tm, tn = _tile(t, 1024), _tile(n, 1024, LANES)
    return pl.pallas_call(
        _linear_kernel,
        grid=(t // tm, n // tn),
        in_specs=[pl.BlockSpec((tm, k), lambda i, j: (i, 0)), pl.BlockSpec((k, tn), lambda i, j: (0, j))],
        out_specs=pl.BlockSpec((tm, tn), lambda i, j: (i, j)),
        out_shape=jax.ShapeDtypeStruct((t, n), F32),
        compiler_params=_params("parallel", "parallel"),
        name="qkv_proj",
    )(h, w)


def _act_linear_kernel(h_ref, w_ref, o_ref, *, n_silu_tiles):
    z = _dot(h_ref[...], w_ref[...])
    sig = jax.nn.sigmoid(z)
    o_ref[...] = jnp.where(pl.program_id(1) < n_silu_tiles, z * sig, sig).astype(o_ref.dtype)


def _act_linear(h, w, n_silu):
    t, k = h.shape
    n = w.shape[1]
    tm, tn = _tile(t, 1024), _tile(math.gcd(n, n_silu), 1024, LANES)
    return pl.pallas_call(
        functools.partial(_act_linear_kernel, n_silu_tiles=n_silu // tn),
        grid=(t // tm, n // tn),
        in_specs=[pl.BlockSpec((tm, k), lambda i, j: (i, 0)), pl.BlockSpec((k, tn), lambda i, j: (0, j))],
        out_specs=pl.BlockSpec((tm, tn), lambda i, j: (i, j)),
        out_shape=jax.ShapeDtypeStruct((t, n), BF16),
        compiler_params=_params("parallel", "parallel"),
        name="gate_proj",
    )(h, w)


def _decay_kernel(h_ref, wa_ref, wa2_ref, ba_ref, o_ref):
    za = _dot(h_ref[...], wa_ref[...])
    x = jnp.dot(za, wa2_ref[...], preferred_element_type=F32, precision=lax.Precision.HIGHEST) + ba_ref[...]
    o_ref[...] = (jnp.minimum(x, 0.0) - jnp.log1p(jnp.exp(-jnp.abs(x)))) * (1.0 / GATE_TEMP)


def _decay_proj(h, wa, wa2, ba):
    t, k = h.shape
    rp = wa.shape[1]
    n = wa2.shape[1]
    tm = _tile(t, 512)
    return pl.pallas_call(
        _decay_kernel,
        grid=(t // tm,),
        in_specs=[pl.BlockSpec((tm, k), lambda i: (i, 0)),
                  pl.BlockSpec((k, rp), lambda i: (0, 0)),
                  pl.BlockSpec((rp, n), lambda i: (0, 0)),
                  pl.BlockSpec((1, n), lambda i: (0, 0))],
        out_specs=pl.BlockSpec((tm, n), lambda i: (i, 0)),
        out_shape=jax.ShapeDtypeStruct((t, n), F32),
        compiler_params=_params("parallel"),
        name="decay_proj",
    )(h, wa, wa2, ba.reshape(1, n))


def _conv_kernel(hist_ref, cur_ref, w_ref, b_ref, g_ref, beta_ref, o_ref, ext_scr, *, nb, lt, width, rt, zero_first):
    off = CONV_HIST_ROWS - (width - 1)
    bias, gain, beta = b_ref[...], g_ref[...], beta_ref[...]
    outs = []
    for n in range(nb):
        hist = hist_ref[n]
        if zero_first:
            hist = jnp.where(pl.program_id(1) == 0, 0.0, hist)
        ext_scr[n, 0:CONV_HIST_ROWS, :] = hist
        ext_scr[n, CONV_HIST_ROWS:CONV_HIST_ROWS + lt, :] = cur_ref[n * lt:(n + 1) * lt, :]
        for r0 in range(0, lt, rt):
            acc = ext_scr[n, r0 + off:r0 + off + rt, :] * w_ref[0:1, :]
            for j in range(1, width):
                acc = acc + ext_scr[n, r0 + off + j:r0 + off + j + rt, :] * w_ref[j:j + 1, :]
            c = acc + bias
            mu = jnp.mean(c, axis=-1, keepdims=True)
            d = c - mu
            var = jnp.mean(d * d, axis=-1, keepdims=True)
            y = d * lax.rsqrt(var + EPS) * gain + beta
            outs.append(y * jax.nn.sigmoid(y))
    o_ref[...] = outs[0] if len(outs) == 1 else jnp.concatenate(outs, axis=0)


def _conv_call(hist, hist_spec, u, cur_spec, grid, out_rows, w_dw, b_dw, ln_g, ln_b, nb, lt, zero_first):
    c = u.shape[1]
    width = w_dw.shape[0]
    row = lambda a: a.reshape(1, c)
    const = lambda shape: pl.BlockSpec(shape, lambda *_: (0, 0))
    out_map = (lambda b, i: (b * grid[1] + i, 0)) if len(grid) == 2 else (lambda g: (g, 0))
    return pl.pallas_call(
        functools.partial(_conv_kernel, nb=nb, lt=lt, width=width, rt=_tile(lt, 32), zero_first=zero_first),
        grid=grid,
        in_specs=[hist_spec, cur_spec, const((width, c)), const((1, c)), const((1, c)), const((1, c))],
        out_specs=pl.BlockSpec((nb * lt, c), out_map),
        out_shape=jax.ShapeDtypeStruct((out_rows, c), F32),
        scratch_shapes=[pltpu.VMEM((nb, CONV_HIST_ROWS + lt, c), F32)],
        compiler_params=_params(*(("parallel",) * len(grid))),
        name="conv_branch",
    )(hist, u, w_dw, row(b_dw), row(ln_g), row(ln_b))


def _conv_prompt(u, nseq, seq_len, w_dw, b_dw, ln_g, ln_b):
    c = u.shape[1]
    lt = _tile(seq_len, 256, CONV_HIST_ROWS)
    nt = seq_len // lt
    assert lt % CONV_HIST_ROWS == 0 and u.shape[0] % CONV_HIST_ROWS == 0
    per = lt // CONV_HIST_ROWS
    slabs = u.reshape(u.shape[0] // CONV_HIST_ROWS, CONV_HIST_ROWS, c)
    hist_spec = pl.BlockSpec((1, CONV_HIST_ROWS, c), lambda b, i: (jnp.maximum((b * nt + i) * per - 1, 0), 0, 0))
    cur_spec = pl.BlockSpec((lt, c), lambda b, i: (b * nt + i, 0))
    return _conv_call(slabs, hist_spec, u, cur_spec, (nseq, nt), nseq * seq_len, w_dw, b_dw, ln_g, ln_b, 1, lt, True)


def _conv_sample(u, row0, state, w_dw, b_dw, ln_g, ln_b):
    nseq, hw, c = state.shape
    seq_len = (u.shape[0] - row0) // nseq
    nb = _tile(nseq, 16, 1)
    hist = jnp.pad(state, ((0, 0), (CONV_HIST_ROWS - hw, 0), (0, 0)))
    assert row0 % (nb * seq_len) == 0
    base = row0 // (nb * seq_len)
    hist_spec = pl.BlockSpec((nb, CONV_HIST_ROWS, c), lambda g: (g, 0, 0))
    cur_spec = pl.BlockSpec((nb * seq_len, c), lambda g: (base + g, 0))
    return _conv_call(hist, hist_spec, u, cur_spec, (nseq // nb,), nseq * seq_len, w_dw, b_dw, ln_g, ln_b,
                      nb, seq_len, False)


def _gla_tables(c):
    nlev = int(math.log2(c))
    assert 2 ** nlev == c
    t = np.arange(c)
    tri = (t[None, :] <= t[:, None]).astype(np.float32)
    mats = [tri]
    masks = []
    for l in range(nlev):
        hs = 2 ** l
        mid = (t // (2 * hs)) * (2 * hs) + hs
        mats.append(tri[mid - 1])
        same = (t[:, None] // (2 * hs)) == (t[None, :] // (2 * hs))
        upper = (t % (2 * hs)) >= hs
        masks.append((same & upper[:, None] & ~upper[None, :]).astype(np.float32))
    mats.append(np.ones((c, c), np.float32))
    return jnp.asarray(np.concatenate(mats, axis=0), BF16), jnp.asarray(np.stack(masks)), nlev


def _gla_kernel(q_ref, k_ref, v_ref, lg_ref, r_ref, s0_ref, gn_ref, mall_ref, masks_ref, o_ref, sout_ref, s_scr,
                *, nb, heads, c, nlev, scale):
    chunk = pl.program_id(1)

    @pl.when(chunk == 0)
    def _():
        s_scr[...] = s0_ref[...]

    dk = q_ref.shape[1] // heads
    dv = v_ref.shape[1] // heads
    probs = [(n, h) for n in range(nb) for h in range(heads)]
    tile = lambda ref, n, h, w: ref[n * c:(n + 1) * c, h * w:(h + 1) * w]
    mall = mall_ref[...]
    row_id = lax.broadcasted_iota(jnp.int32, (c, dk), 0)
    eye = lax.broadcasted_iota(jnp.int32, (c, c), 0) == lax.broadcasted_iota(jnp.int32, (c, c), 1)
    ones = jnp.ones((c, LANES), BF16)

    q = [tile(q_ref, n, h, dk) * scale for n, h in probs]
    k = [tile(k_ref, n, h, dk) for n, h in probs]
    v = [tile(v_ref, n, h, dv).astype(BF16) for n, h in probs]
    parts = [_split3(tile(lg_ref, n, h, dk)) for n, h in probs]
    ball = [_dot(mall, p[0]) + _dot(mall, p[1]) + _dot(mall, p[2]) for p in parts]
    b = [x[0:c] for x in ball]
    att = [jnp.where(eye, jnp.sum(qi * ki, axis=-1, keepdims=True), 0.0) for qi, ki in zip(q, k)]
    for l in range(nlev):
        upper = (row_id & (2 ** (l + 1) - 1)) >= 2 ** l
        x = [(jnp.where(upper, qi, ki) * jnp.exp(-jnp.abs(bi - bl[(l + 1) * c:(l + 2) * c]))).astype(BF16)
             for qi, ki, bi, bl in zip(q, k, b, ball)]
        att = [ai + _dot_nt(xi, xi) * masks_ref[l] for ai, xi in zip(att, x)]
    s = [s_scr[n, h] for n, h in probs]
    o = [_dot(ai.astype(BF16), vi) + _dot((qi * jnp.exp(bi)).astype(BF16), si.astype(BF16))
         for ai, vi, qi, bi, si in zip(att, v, q, b, s)]
    kd = [(ki * jnp.exp(bl[(nlev + 1) * c:(nlev + 2) * c] - bi)).astype(BF16) for ki, bi, bl in zip(k, b, ball)]
    b_col = [_dot_tn(p[0], ones) + _dot_tn(p[1], ones) + _dot_tn(p[2], ones) for p in parts]
    for (n, h), bc, si, kdi, vi in zip(probs, b_col, s, kd, v):
        decay = jnp.concatenate([jnp.exp(bc)] * (dv // LANES), axis=1)
        s_scr[n, h] = decay * si + _dot_tn(kdi, vi)
    gate = r_ref[...].astype(F32)
    for h in range(heads):
        cols = slice(h * dv, (h + 1) * dv)
        outs = []
        for n in range(nb):
            oi = o[n * heads + h]
            on = oi * lax.rsqrt(jnp.mean(oi * oi, axis=-1, keepdims=True) + EPS)
            outs.append(on * gn_ref[:, cols] * gate[n * c:(n + 1) * c, cols])
        o_ref[:, cols] = (outs[0] if nb == 1 else jnp.concatenate(outs, axis=0)).astype(o_ref.dtype)

    @pl.when(chunk == pl.num_programs(1) - 1)
    def _():
        sout_ref[...] = s_scr[...]


def _gla_branch(zqkv, lg, zrgg, s0, gn, *, row0, seq_len, nb):
    nseq, heads, dk, dv = s0.shape
    gk, gv = heads * dk, heads * dv
    c = min(GLA_CHUNK, seq_len)
    nc = seq_len // c
    assert seq_len % c == 0 and (nb == 1 or nc == 1) and nseq % nb == 0
    rows = nb * c
    assert row0 % rows == 0 and (2 * gk) % gv == 0 and dv % LANES == 0
    base = row0 // rows
    mall, masks, nlev = _gla_tables(c)
    rmap = lambda s, ch: base + s * nc + ch
    return pl.pallas_call(
        functools.partial(_gla_kernel, nb=nb, heads=heads, c=c, nlev=nlev, scale=dk ** -0.5),
        grid=(nseq // nb, nc),
        in_specs=[pl.BlockSpec((rows, gk), lambda s, ch: (rmap(s, ch), 0)),
                  pl.BlockSpec((rows, gk), lambda s, ch: (rmap(s, ch), 1)),
                  pl.BlockSpec((rows, gv), lambda s, ch: (rmap(s, ch), 2 * gk // gv)),
                  pl.BlockSpec((rows, gk), lambda s, ch: (rmap(s, ch), 0)),
                  pl.BlockSpec((rows, gv), lambda s, ch: (rmap(s, ch), 0)),
                  pl.BlockSpec((nb, heads, dk, dv), lambda s, ch: (s, 0, 0, 0)),
                  pl.BlockSpec((1, gv), lambda s, ch: (0, 0)),
                  pl.BlockSpec(mall.shape, lambda s, ch: (0, 0)),
                  pl.BlockSpec(masks.shape, lambda s, ch: (0, 0, 0))],
        out_specs=[pl.BlockSpec((rows, gv), lambda s, ch: (s * nc + ch, 0)),
                   pl.BlockSpec((nb, heads, dk, dv), lambda s, ch: (s, 0, 0, 0))],
        out_shape=[jax.ShapeDtypeStruct((nseq * seq_len, gv), BF16),
                   jax.ShapeDtypeStruct((nseq, heads, dk, dv), F32)],
        scratch_shapes=[pltpu.VMEM((nb, heads, dk, dv), F32)],
        compiler_params=_params("parallel", "arbitrary"),
        name="gla_branch",
    )(zqkv, zqkv, zqkv, lg, zrgg, s0, gn.reshape(1, gv), mall, masks)


def _merge_kernel(cp_ref, cs_ref, ogp_ref, ogs_ref, wc_ref, wg_ref, ga_ref, gb_ref, o_ref, *, n_first):
    def emit(refs):
        c_ref, og_ref = refs
        a = _dot(c_ref[...].astype(BF16), wc_ref[...])
        b = _dot(og_ref[...], wg_ref[...])
        o_ref[...] = (ga_ref[...].astype(F32) * a + gb_ref[...].astype(F32) * b).astype(o_ref.dtype)

    _pick_part(n_first, (cp_ref, ogp_ref), (cs_ref, ogs_ref), emit)


def _merge(c_p, c_s, og_p, og_s, wc, wg, zrgg, ga_col0, gb_col0):
    (tp, kc), ts = c_p.shape, c_s.shape[0]
    kg = og_p.shape[1]
    n = wc.shape[1]
    tm, tn = _tile(math.gcd(tp, ts), 512), _tile(math.gcd(math.gcd(n, ga_col0), gb_col0), 1024, LANES)
    nf = tp // tm
    return pl.pallas_call(
        functools.partial(_merge_kernel, n_first=nf),
        grid=((tp + ts) // tm, n // tn),
        in_specs=[*_two_part_specs(tm, kc, nf, 2), *_two_part_specs(tm, kg, nf, 2),
                  pl.BlockSpec((kc, tn), lambda i, j: (0, j)),
                  pl.BlockSpec((kg, tn), lambda i, j: (0, j)),
                  pl.BlockSpec((tm, tn), lambda i, j: (i, ga_col0 // tn + j)),
                  pl.BlockSpec((tm, tn), lambda i, j: (i, gb_col0 // tn + j))],
        out_specs=pl.BlockSpec((tm, tn), lambda i, j: (i, j)),
        out_shape=jax.ShapeDtypeStruct((tp + ts, n), BF16),
        compiler_params=_params("parallel", "parallel"),
        name="merge_proj",
    )(c_p, c_s, og_p, og_s, wc, wg, zrgg, zrgg)


def _mix_kernel(m_ref, w_ref, xp_ref, xs_ref, g_ref, x2_ref, h2t_ref, *, n_first):
    def emit(x_ref):
        x2 = x_ref[...] + _dot(m_ref[...], w_ref[...])
        x2_ref[...] = x2
        y = x2 * lax.rsqrt(jnp.mean(x2 * x2, axis=-1, keepdims=True) + EPS)
        h2t_ref[...] = (y * g_ref[...]).T.astype(h2t_ref.dtype)

    _pick_part(n_first, xp_ref, xs_ref, emit)


def _mix(merged, w, xp, xs, g):
    (tp, d), ts = xp.shape, xs.shape[0]
    k = merged.shape[1]
    tm = _tile(math.gcd(tp, ts), 256, LANES)
    return pl.pallas_call(
        functools.partial(_mix_kernel, n_first=tp // tm),
        grid=((tp + ts) // tm,),
        in_specs=[pl.BlockSpec((tm, k), lambda i: (i, 0)),
                  pl.BlockSpec((k, d), lambda i: (0, 0)),
                  *_two_part_specs(tm, d, tp // tm),
                  pl.BlockSpec((1, d), lambda i: (0, 0))],
        out_specs=[pl.BlockSpec((tm, d), lambda i: (i, 0)), pl.BlockSpec((d, tm), lambda i: (0, i))],
        out_shape=[jax.ShapeDtypeStruct((tp + ts, d), F32), jax.ShapeDtypeStruct((d, tp + ts), BF16)],
        compiler_params=_params("parallel"),
        name="mix_proj",
    )(merged, w, xp, xs, g.reshape(1, d))


def _score_kernel(h2t_ref, wqt_ref, k1_ref, k2_ref, s_ref, *, heads, half):
    qt = _dot(wqt_ref[...], h2t_ref[...])
    k1, k2 = k1_ref[...], k2_ref[...]
    for h in range(heads):
        lo = 2 * half * h
        s_ref[2 * h] = _dot(k1, qt[lo:lo + half].astype(BF16))
        s_ref[2 * h + 1] = _dot(k2, qt[lo + half:lo + 2 * half].astype(BF16))


def _peer_scores(h2t, wqt, k1, k2, heads):
    d, t = h2t.shape
    nq = wqt.shape[0]
    nkeys, half = k1.shape
    tl = _tile(t, 512, LANES)
    return pl.pallas_call(
        functools.partial(_score_kernel, heads=heads, half=half),
        grid=(t // tl,),
        in_specs=[pl.BlockSpec((d, tl), lambda i: (0, i)),
                  pl.BlockSpec((nq, d), lambda i: (0, 0)),
                  pl.BlockSpec((nkeys, half), lambda i: (0, 0)),
                  pl.BlockSpec((nkeys, half), lambda i: (0, 0))],
        out_specs=pl.BlockSpec((2 * heads, nkeys, tl), lambda i: (0, 0, i)),
        out_shape=jax.ShapeDtypeStruct((2 * heads, nkeys, t), F32),
        compiler_params=_params("parallel"),
        name="peer_scores",
    )(h2t, wqt, k1, k2)


def _top_ranks(arrays, k, first_index_ties):
    tl = arrays[0].shape[1]
    idx = [lax.broadcasted_iota(jnp.int32, a.shape, 0).astype(F32) for a in arrays]
    slot = lax.broadcasted_iota(jnp.int32, (k, tl), 0)

    def body(r, carry):
        out = []
        for (work, rank, vals), ix in zip(carry, idx):
            m = jnp.max(work, axis=0, keepdims=True)
            if first_index_ties:
                first = jnp.min(jnp.where(work == m, ix, float(work.shape[0])), axis=0, keepdims=True)
                sel = ix == first
            else:
                sel = work == m
            rank = jnp.where(sel, lax.convert_element_type(r, F32), rank)
            work = jnp.where(sel, -jnp.inf, work)
            vals = jnp.where(slot == r, m, vals)
            out.append((work, rank, vals))
        return tuple(out)

    init = tuple((a, jnp.full(a.shape, float(k), F32), jnp.zeros((k, tl), F32)) for a in arrays)
    return [(rank, vals) for _, rank, vals in lax.fori_loop(0, k, body, init)]


def _all_ranked(rank, k):
    n = jnp.sum(jnp.where(rank < float(k), 1.0, 0.0), axis=0, keepdims=True)
    return jnp.where(n == float(k), 1.0, 0.0)


def _candidate_tables(k):
    pairs = [(r, c) for r in range(k) for c in range(k) if (r + 1) * (c + 1) <= k]
    rows = -(-len(pairs) // 8) * 8
    sel_r = np.zeros((rows, k), np.float32)
    sel_c = np.zeros((rows, k), np.float32)
    pad = np.zeros((rows, LANES), np.float32)
    for i, (r, c) in enumerate(pairs):
        sel_r[i, r] = 1.0
        sel_c[i, c] = 1.0
    pad[len(pairs):] = -np.inf
    return (jnp.asarray(sel_r, BF16), jnp.asarray(sel_c, BF16), jnp.asarray(sel_r.T.copy(), BF16), jnp.asarray(pad))


def _pick(sel, x):
    hi, mid, lo = _split3(x)
    return _dot(sel, hi) + _dot(sel, mid) + _dot(sel, lo)


def _select_kernel(s_ref, selr_ref, selc_ref, selrt_ref, pad_ref, rank2_ref, cnt1_ref, p1_ref, p2_ref, *, heads, topk):
    def run(first_index_ties):
        def per_head(h, ok):
            s1 = s_ref[2 * h]
            s2 = s_ref[2 * h + 1]
            (rank1, v1), (rank2, v2) = _top_ranks([s1, s2], topk, first_index_ties)
            cand = _pick(selr_ref[...], v1) + _pick(selc_ref[...], v2) + pad_ref[...]
            (rank_c, cs), = _top_ranks([cand], topk, first_index_ties)
            taken = jnp.where(rank_c < float(topk), 1.0, 0.0).astype(BF16)
            per_row = _dot(selrt_ref[...], taken)
            z = jnp.sum(jnp.exp(cs - cs[0:1]), axis=0, keepdims=True)
            cnt1 = jnp.zeros_like(s1)
            for r in range(topk):
                cnt1 = cnt1 + jnp.where(rank1 == float(r), per_row[r:r + 1], 0.0)
            rank2_ref[h, 0] = rank2
            cnt1_ref[h, 0] = cnt1
            p1_ref[h, 0] = jnp.exp(s1 - v1[0:1]) / z
            p2_ref[h, 0] = jnp.exp(s2 - v2[0:1])
            if first_index_ties:
                return ok
            return ok * _all_ranked(rank1, topk) * _all_ranked(rank2, topk) * _all_ranked(rank_c, topk)

        return lax.fori_loop(0, heads, per_head, jnp.ones((1, s_ref.shape[2]), F32))

    ok = run(False)

    @pl.when(jnp.min(ok) < 0.5)
    def _():
        run(True)


def _peer_select(scores, heads):
    _, nkeys, t = scores.shape
    assert t % LANES == 0
    shape = lambda dt: jax.ShapeDtypeStruct((heads, t // LANES, nkeys, LANES), dt)
    spec = pl.BlockSpec((heads, 1, nkeys, LANES), lambda i: (0, i, 0, 0))
    tables = _candidate_tables(PEER_TOPK)
    return pl.pallas_call(
        functools.partial(_select_kernel, heads=heads, topk=PEER_TOPK),
        grid=(t // LANES,),
        in_specs=[pl.BlockSpec((2 * heads, nkeys, LANES), lambda i: (0, 0, i))]
        + [pl.BlockSpec(tb.shape, lambda i: (0, 0)) for tb in tables],
        out_specs=[spec] * 4,
        out_shape=[shape(F32)] * 4,
        compiler_params=_params("parallel"),
        name="peer_select",
    )(scores, *tables)


def _peer_kernel(h2t_ref, u_ref, vt_ref, rank2_ref, cnt1_ref, p1_ref, p2_ref, x2_ref, fg_ref, yp_ref, ys_ref, acc_ref,
                 act0_ref, act1_ref, w0_ref, w1_ref, *, heads, nkeys, n1, n_first):
    s = pl.program_id(1)
    last = pl.num_programs(1) - 1
    ne = last - 1

    @pl.when(s == 0)
    def _():
        acc_ref[...] = jnp.zeros_like(acc_ref)
        act1_ref[...] = jnp.zeros_like(act1_ref)
        w0_ref[...] = jnp.zeros_like(w0_ref)

    def step(act_prev, w_next, act_next, w_prev):
        blk = jnp.clip(s - 1, 0, ne - 1)
        nq = act_prev.shape[1] // LANES
        d, dm = acc_ref.shape[0], u_ref.shape[1]

        def gate_piece(q, j):
            rows, lanes = slice(j * nkeys, (j + 1) * nkeys), slice(q * LANES, (q + 1) * LANES)
            i1 = blk * n1 + j
            g = None
            for h in range(heads):
                c1 = cnt1_ref[h, q, pl.ds(i1, 1), :]
                w1 = p1_ref[h, q, pl.ds(i1, 1), :]
                term = jnp.where(rank2_ref[h, q] < c1, p2_ref[h, q], 0.0) * w1
                g = term if g is None else g + term
            w_next[rows, lanes] = (g * jax.nn.gelu(act_prev[rows, lanes])).astype(BF16)

        def act_piece(k):
            ks = slice(k * dm // nq, (k + 1) * dm // nq)
            part = _dot(u_ref[:, ks], h2t_ref[ks, :])
            if k == 0:
                act_next[...] = part
            else:
                act_next[...] += part

        def out_piece(r):
            rs = slice(r * d // nq, (r + 1) * d // nq)
            acc_ref[rs, :] += _dot(vt_ref[rs, :], w_prev[...])

        for q in range(nq):
            for j in range(n1):
                gate_piece(q, j)
                if j == n1 // 2 - 1 or n1 == 1:
                    act_piece(q)
            out_piece(q)

    @pl.when(s % 2 == 0)
    def _():
        step(act1_ref, w1_ref, act0_ref, w0_ref)

    @pl.when(s % 2 == 1)
    def _():
        step(act0_ref, w0_ref, act1_ref, w1_ref)

    def finish(y_ref):
        y = x2_ref[...] + acc_ref[...].T
        y_ref[...] = y * lax.rsqrt(jnp.mean(y * y, axis=-1, keepdims=True) + EPS) * fg_ref[...]

    @pl.when(s == last)
    def _():
        _pick_part(n_first, yp_ref, ys_ref, finish)


def _peer_dense(h2t, u, vt, rank2, cnt1, p1, p2, x2, fg, t_first):
    d, t = h2t.shape
    n_exp = u.shape[0]
    heads, _, nkeys, _ = rank2.shape
    tb = _tile(math.gcd(t_first, t - t_first), 512, LANES)
    assert tb % LANES == 0
    nf = t_first // tb
    n1 = max(1, min(512, n_exp) // nkeys)
    eb = n1 * nkeys
    ne = n_exp // eb
    sel = pl.BlockSpec((heads, tb // LANES, nkeys, LANES), lambda i, e: (0, i, 0, 0))
    return pl.pallas_call(
        functools.partial(_peer_kernel, heads=heads, nkeys=nkeys, n1=n1, n_first=nf),
        grid=(t // tb, ne + 2),
        in_specs=[pl.BlockSpec((d, tb), lambda i, e: (0, i)),
                  pl.BlockSpec((eb, d), lambda i, e: (jnp.minimum(e, ne - 1), 0)),
                  pl.BlockSpec((d, eb), lambda i, e: (0, jnp.clip(e - 2, 0, ne - 1))),
                  sel, sel, sel, sel,
                  pl.BlockSpec((tb, d), lambda i, e: (i, 0), pipeline_mode=pl.Buffered(1)),
                  pl.BlockSpec((1, d), lambda i, e: (0, 0))],
        out_specs=[pl.BlockSpec((tb, d), lambda i, e: (jnp.minimum(i, nf - 1), 0), pipeline_mode=pl.Buffered(1)),
                   pl.BlockSpec((tb, d), lambda i, e: (jnp.maximum(i - nf, 0), 0), pipeline_mode=pl.Buffered(1))],
        out_shape=[jax.ShapeDtypeStruct((t_first, d), F32), jax.ShapeDtypeStruct((t - t_first, d), F32)],
        scratch_shapes=[pltpu.VMEM((d, tb), F32), pltpu.VMEM((eb, tb), F32), pltpu.VMEM((eb, tb), F32),
                        pltpu.VMEM((eb, tb), BF16), pltpu.VMEM((eb, tb), BF16)],
        compiler_params=_params("parallel", "arbitrary"),
        name="peer_dense",
    )(h2t, u, vt, rank2, cnt1, p1, p2, x2, fg.reshape(1, d))


def kernel(x_prompt, x_sample, state_conv, state_gla, norm1_g, w_in, w_a2, b_a, w_dw, b_dw, conv_ln_g, conv_ln_b, w_conv_out, gla_norm_g, w_gla_o, w_mix_out, norm2_g, peer_wq, peer_k1, peer_k2, peer_u, peer_v, final_g):
    assert w_in.shape[0] == 1, "one layer"
    bp, lp, d = x_prompt.shape
    bs, ls, _ = x_sample.shape
    tp, ts = bp * lp, bs * ls
    cc = state_conv.shape[-1]
    width = w_dw.shape[1]
    _, _, heads, dk, dv = state_gla.shape
    gk, gv = heads * dk, heads * dv
    rank = w_a2.shape[1]
    nkeys, half = peer_k1.shape[1:]
    pheads = peer_wq.shape[2] // (2 * half)

    xp, xs = x_prompt.reshape(tp, d), x_sample.reshape(ts, d)

    w = w_in[0].astype(BF16)
    o = 0
    w_ca, o = w[:, o:o + cc], o + cc
    w_cg, o = w[:, o:o + cc], o + cc
    w_qkv, o = w[:, o:o + 2 * gk + gv], o + 2 * gk + gv
    w_r, o = w[:, o:o + gv], o + gv
    w_a, o = w[:, o:o + rank], o + rank
    w_gab = w[:, o:o + 2 * d]
    rp = -(-rank // LANES) * LANES
    w_a = jnp.pad(w_a, ((0, 0), (0, rp - rank)))
    w_a2p = jnp.pad(w_a2[0], ((0, rp - rank), (0, 0)))

    h = _rmsnorm(xp, xs, norm1_g[0], BF16)
    u = _glu_proj(h, w_ca, w_cg)
    zqkv = _linear(h, w_qkv)
    zrgg = _act_linear(h, jnp.concatenate([w_r, w_gab], axis=1), gv)
    lg = _decay_proj(h, w_a, w_a2p, b_a[0])

    conv_w = (w_dw[0], b_dw[0], conv_ln_g[0], conv_ln_b[0])
    c_p = _conv_prompt(u, bp, lp, *conv_w)
    c_s = _conv_sample(u, tp, state_conv[0], *conv_w)
    tail = lambda hist, cur: jnp.concatenate([hist, cur], axis=1)[:, -(width - 1):]
    conv_p = tail(jnp.zeros((bp, width - 1, cc), F32), u[:tp].reshape(bp, lp, cc)[:, -min(lp, width - 1):])
    conv_s = tail(state_conv[0], u[tp:].reshape(bs, ls, cc))

    gla = functools.partial(_gla_branch, zqkv, lg, zrgg, gn=gla_norm_g[0])
    og_p, gla_p = gla(jnp.zeros((bp, heads, dk, dv), F32), row0=0, seq_len=lp, nb=1)
    og_s, gla_s = gla(state_gla[0], row0=tp, seq_len=ls, nb=_tile(bs, 2, 2) if ls < GLA_CHUNK else 1)
    merged = _merge(c_p, c_s, og_p, og_s, w_conv_out[0].astype(BF16), w_gla_o[0].astype(BF16), zrgg, gv, gv + d)
    x2, h2t = _mix(merged, w_mix_out[0].astype(BF16), xp, xs, norm2_g[0])

    scores = _peer_scores(h2t, peer_wq[0].T.astype(BF16), peer_k1[0].astype(BF16), peer_k2[0].astype(BF16), pheads)
    rank2, cnt1, p1, p2 = _peer_select(scores, pheads)
    y_p, y_s = _peer_dense(h2t, peer_u[0].astype(BF16), peer_v[0].T.astype(BF16), rank2, cnt1, p1, p2, x2, final_g, tp)

    return (y_p.reshape(bp, lp, d), y_s.reshape(bs, ls, d),
            conv_p[None].astype(state_conv.dtype), gla_p[None].astype(state_gla.dtype),
            conv_s[None].astype(state_conv.dtype), gla_s[None].astype(state_gla.dtype))
```

```python
import functools
import math

import numpy as np
import jax
import jax.numpy as jnp
from jax import lax
from jax.experimental import pallas as pl
from jax.experimental.pallas import tpu as pltpu

EPS = 1e-6
GLA_CHUNK = 64
GATE_TEMP = 16.0
PEER_TOPK = 16
CONV_HIST_ROWS = 32
LANES = 128
VMEM_LIMIT_BYTES = 56 * 1024 * 1024

BF16 = jnp.bfloat16
F32 = jnp.float32


def _tile(n, pref, mult=8):
    if n <= pref:
        return n
    for t in range(pref - pref % mult, 0, -mult):
        if n % t == 0:
            return t
    return n


def _params(*sem):
    return pltpu.CompilerParams(dimension_semantics=sem, vmem_limit_bytes=VMEM_LIMIT_BYTES)


def _dot(a, b):
    return jnp.dot(a, b, preferred_element_type=F32)


def _dot_tn(a, b):
    return lax.dot_general(a, b, (((0,), (0,)), ((), ())), preferred_element_type=F32)


def _dot_nt(a, b):
    return lax.dot_general(a, b, (((1,), (1,)), ((), ())), preferred_element_type=F32)


def _split3(x):
    hi = x.astype(BF16)
    r1 = x - hi.astype(F32)
    mid = r1.astype(BF16)
    lo = (r1 - mid.astype(F32)).astype(BF16)
    return hi, mid, lo


def _two_part_specs(tm, cols, n_first, grid_rank=1):
    if grid_rank == 1:
        return (pl.BlockSpec((tm, cols), lambda i: (jnp.minimum(i, n_first - 1), 0)),
                pl.BlockSpec((tm, cols), lambda i: (jnp.maximum(i - n_first, 0), 0)))
    return (pl.BlockSpec((tm, cols), lambda i, j: (jnp.minimum(i, n_first - 1), 0)),
            pl.BlockSpec((tm, cols), lambda i, j: (jnp.maximum(i - n_first, 0), 0)))


def _pick_part(n_first, first_ref, second_ref, emit):
    i = pl.program_id(0)

    @pl.when(i < n_first)
    def _():
        emit(first_ref)

    @pl.when(i >= n_first)
    def _():
        emit(second_ref)


def _rmsnorm_kernel(xp_ref, xs_ref, g_ref, o_ref, *, n_first):
    def emit(x_ref):
        x = x_ref[...]
        y = x * lax.rsqrt(jnp.mean(x * x, axis=-1, keepdims=True) + EPS)
        o_ref[...] = (y * g_ref[...]).astype(o_ref.dtype)

    _pick_part(n_first, xp_ref, xs_ref, emit)


def _rmsnorm(xp, xs, g, out_dtype):
    (tp, d), ts = xp.shape, xs.shape[0]
    tm = _tile(math.gcd(tp, ts), 512)
    return pl.pallas_call(
        functools.partial(_rmsnorm_kernel, n_first=tp // tm),
        grid=((tp + ts) // tm,),
        in_specs=[*_two_part_specs(tm, d, tp // tm), pl.BlockSpec((1, d), lambda i: (0, 0))],
        out_specs=pl.BlockSpec((tm, d), lambda i: (i, 0)),
        out_shape=jax.ShapeDtypeStruct((tp + ts, d), out_dtype),
        compiler_params=_params("parallel"),
        name="rmsnorm",
    )(xp, xs, g.reshape(1, d))


def _glu_kernel(h_ref, wa_ref, wg_ref, o_ref):
    h = h_ref[...]
    o_ref[...] = _dot(h, wa_ref[...]) * jax.nn.sigmoid(_dot(h, wg_ref[...]))


def _glu_proj(h, wa, wg):
    t, k = h.shape
    n = wa.shape[1]
    tm, tn = _tile(t, 1024), _tile(n, 512, LANES)
    return pl.pallas_call(
        _glu_kernel,
        grid=(t // tm, n // tn),
        in_specs=[pl.BlockSpec((tm, k), lambda i, j: (i, 0)),
                  pl.BlockSpec((k, tn), lambda i, j: (0, j)),
                  pl.BlockSpec((k, tn), lambda i, j: (0, j))],
        out_specs=pl.BlockSpec((tm, tn), lambda i, j: (i, j)),
        out_shape=jax.ShapeDtypeStruct((t, n), F32),
        compiler_params=_params("parallel", "parallel"),
        name="glu_proj",
    )(h, wa, wg)


def _linear_kernel(h_ref, w_ref, o_ref):
    o_ref[...] = _dot(h_ref[...], w_ref[...])


def _linear(h, w):
    t, k = h.shape
    n = w.shape[1]
    tm, tn = _tile(t, 1024), _tile(n, 1024, LANES)
    return pl.pallas_call(
        _linear_kernel,
        grid=(t // tm, n // tn),
        in_specs=[pl.BlockSpec((tm, k), lambda i, j: (i, 0)), pl.BlockSpec((k, tn), lambda i, j: (0, j))],
        out_specs=pl.BlockSpec((tm, tn), lambda i, j: (i, j)),
        out_shape=jax.ShapeDtypeStruct((t, n), F32),
        compiler_params=_params("parallel", "parallel"),
        name="qkv_proj",
    )(h, w)


def _act_linear_kernel(h_ref, w_ref, o_ref, *, n_silu_tiles):
    z = _dot(h_ref[...], w_ref[...])
    sig = jax.nn.sigmoid(z)
    o_ref[...] = jnp.where(pl.program_id(1) < n_silu_tiles, z * sig, sig).astype(o_ref.dtype)


def _act_linear(h, w, n_silu):
    t, k = h.shape
    n = w.shape[1]
    tm, tn = _tile(t, 1024), _tile(math.gcd(n, n_silu), 1024, LANES)
    return pl.pallas_call(
        functools.partial(_act_linear_kernel, n_silu_tiles=n_silu // tn),
        grid=(t // tm, n // tn),
        in_specs=[pl.BlockSpec((tm, k), lambda i, j: (i, 0)), pl.BlockSpec((k, tn), lambda i, j: (0, j))],
        out_specs=pl.BlockSpec((tm, tn), lambda i, j: (i, j)),
        out_shape=jax.ShapeDtypeStruct((t, n), BF16),
        compiler_params=_params("parallel", "parallel"),
        name="gate_proj",
    )(h, w)


def _decay_kernel(h_ref, wa_ref, wa2_ref, ba_ref, o_ref):
    za = _dot(h_ref[...], wa_ref[...])
    x = jnp.dot(za, wa2_ref[...], preferred_element_type=F32, precision=lax.Precision.HIGHEST) + ba_ref[...]
    o_ref[...] = (jnp.minimum(x, 0.0) - jnp.log1p(jnp.exp(-jnp.abs(x)))) * (1.0 / GATE_TEMP)


def _decay_proj(h, wa, wa2, ba):
    t, k = h.shape
    rp = wa.shape[1]
    n = wa2.shape[1]
    tm = _tile(t, 512)
    return pl.pallas_call(
        _decay_kernel,
        grid=(t // tm,),
        in_specs=[pl.BlockSpec((tm, k), lambda i: (i, 0)),
                  pl.BlockSpec((k, rp), lambda i: (0, 0)),
                  pl.BlockSpec((rp, n), lambda i: (0, 0)),
                  pl.BlockSpec((1, n), lambda i: (0, 0))],
        out_specs=pl.BlockSpec((tm, n), lambda i: (i, 0)),
        out_shape=jax.ShapeDtypeStruct((t, n), F32),
        compiler_params=_params("parallel"),
        name="decay_proj",
    )(h, wa, wa2, ba.reshape(1, n))


def _conv_kernel(hist_ref, cur_ref, w_ref, b_ref, g_ref, beta_ref, o_ref, ext_scr, *shift_scr,
                 nb, lt, width, rt, zero_first):
    off = CONV_HIST_ROWS - (width - 1)
    bias, gain, beta = b_ref[...], g_ref[...], beta_ref[...]
    outs = []
    for n in range(nb):
        hist = hist_ref[n]
        if zero_first:
            hist = jnp.where(pl.program_id(1) == 0, 0.0, hist)
        ext_scr[n, 0:CONV_HIST_ROWS, :] = hist
        ext_scr[n, CONV_HIST_ROWS:CONV_HIST_ROWS + lt, :] = cur_ref[n * lt:(n + 1) * lt, :]
        if shift_scr:
            span = shift_scr[0].shape[1]
            for s in range(1, 8):
                shift_scr[0][s - 1] = ext_scr[n, s:s + span, :]

        def window(start):
            if shift_scr and start % 8:
                return shift_scr[0][start % 8 - 1, start - start % 8:start - start % 8 + rt, :]
            return ext_scr[n, start:start + rt, :]

        for r0 in range(0, lt, rt):
            acc = window(r0 + off) * w_ref[0:1, :]
            for j in range(1, width):
                acc = acc + window(r0 + off + j) * w_ref[j:j + 1, :]
            c = acc + bias
            mu = jnp.mean(c, axis=-1, keepdims=True)
            d = c - mu
            var = jnp.mean(d * d, axis=-1, keepdims=True)
            y = d * lax.rsqrt(var + EPS) * gain + beta
            outs.append(y * jax.nn.sigmoid(y))
    o_ref[...] = outs[0] if len(outs) == 1 else jnp.concatenate(outs, axis=0)


def _conv_call(hist, hist_spec, u, cur_spec, grid, out_rows, w_dw, b_dw, ln_g, ln_b, nb, lt, zero_first):
    c = u.shape[1]
    width = w_dw.shape[0]
    row = lambda a: a.reshape(1, c)
    const = lambda shape: pl.BlockSpec(shape, lambda *_: (0, 0))
    out_map = (lambda b, i: (b * grid[1] + i, 0)) if len(grid) == 2 else (lambda g: (g, 0))
    return pl.pallas_call(
        functools.partial(_conv_kernel, nb=nb, lt=lt, width=width, rt=_tile(lt, 32), zero_first=zero_first),
        grid=grid,
        in_specs=[hist_spec, cur_spec, const((width, c)), const((1, c)), const((1, c)), const((1, c))],
        out_specs=pl.BlockSpec((nb * lt, c), out_map),
        out_shape=jax.ShapeDtypeStruct((out_rows, c), F32),
        scratch_shapes=[pltpu.VMEM((nb, CONV_HIST_ROWS + lt, c), F32)]
        + ([pltpu.VMEM((7, lt + CONV_HIST_ROWS - 8, c), F32)] if nb == 1 and lt >= 64 else []),
        compiler_params=_params(*(("parallel",) * len(grid))),
        name="conv_branch",
    )(hist, u, w_dw, row(b_dw), row(ln_g), row(ln_b))


def _conv_prompt(u, nseq, seq_len, w_dw, b_dw, ln_g, ln_b):
    c = u.shape[1]
    lt = _tile(seq_len, 256, CONV_HIST_ROWS)
    nt = seq_len // lt
    assert lt % CONV_HIST_ROWS == 0 and u.shape[0] % CONV_HIST_ROWS == 0
    per = lt // CONV_HIST_ROWS
    slabs = u.reshape(u.shape[0] // CONV_HIST_ROWS, CONV_HIST_ROWS, c)
    hist_spec = pl.BlockSpec((1, CONV_HIST_ROWS, c), lambda b, i: (jnp.maximum((b * nt + i) * per - 1, 0), 0, 0))
    cur_spec = pl.BlockSpec((lt, c), lambda b, i: (b * nt + i, 0))
    return _conv_call(slabs, hist_spec, u, cur_spec, (nseq, nt), nseq * seq_len, w_dw, b_dw, ln_g, ln_b, 1, lt, True)


def _conv_sample(u, row0, state, w_dw, b_dw, ln_g, ln_b):
    nseq, hw, c = state.shape
    seq_len = (u.shape[0] - row0) // nseq
    nb = _tile(nseq, 16, 1)
    hist = jnp.pad(state, ((0, 0), (CONV_HIST_ROWS - hw, 0), (0, 0)))
    assert row0 % (nb * seq_len) == 0
    base = row0 // (nb * seq_len)
    hist_spec = pl.BlockSpec((nb, CONV_HIST_ROWS, c), lambda g: (g, 0, 0))
    cur_spec = pl.BlockSpec((nb * seq_len, c), lambda g: (base + g, 0))
    return _conv_call(hist, hist_spec, u, cur_spec, (nseq // nb,), nseq * seq_len, w_dw, b_dw, ln_g, ln_b,
                      nb, seq_len, False)


def _gla_tables(c):
    nlev = int(math.log2(c))
    assert 2 ** nlev == c
    t = np.arange(c)
    tri = (t[None, :] <= t[:, None]).astype(np.float32)
    mats = [tri]
    masks = []
    for l in range(nlev):
        hs = 2 ** l
        mid = (t // (2 * hs)) * (2 * hs) + hs
        mats.append(tri[mid - 1])
        same = (t[:, None] // (2 * hs)) == (t[None, :] // (2 * hs))
        upper = (t % (2 * hs)) >= hs
        masks.append((same & upper[:, None] & ~upper[None, :]).astype(np.float32))
    mats.append(np.ones((c, c), np.float32))
    return jnp.asarray(np.concatenate(mats, axis=0), BF16), jnp.asarray(np.stack(masks)), nlev


def _gla_kernel(q_ref, k_ref, v_ref, lg_ref, r_ref, s0_ref, gn_ref, mall_ref, masks_ref, o_ref, sout_ref, s_scr,
                *, nb, heads, c, nlev, scale):
    chunk = pl.program_id(1)

    @pl.when(chunk == 0)
    def _():
        s_scr[...] = s0_ref[...]

    dk = q_ref.shape[1] // heads
    dv = v_ref.shape[1] // heads
    probs = [(n, h) for n in range(nb) for h in range(heads)]
    tile = lambda ref, n, h, w: ref[n * c:(n + 1) * c, h * w:(h + 1) * w]
    mall = mall_ref[...]
    row_id = lax.broadcasted_iota(jnp.int32, (c, dk), 0)
    eye = lax.broadcasted_iota(jnp.int32, (c, c), 0) == lax.broadcasted_iota(jnp.int32, (c, c), 1)
    ones = jnp.ones((c, LANES), BF16)

    q = [tile(q_ref, n, h, dk) * scale for n, h in probs]
    k = [tile(k_ref, n, h, dk) for n, h in probs]
    v = [tile(v_ref, n, h, dv).astype(BF16) for n, h in probs]
    parts = [_split3(tile(lg_ref, n, h, dk)) for n, h in probs]
    ball = [_dot(mall, p[0]) + _dot(mall, p[1]) + _dot(mall, p[2]) for p in parts]
    b = [x[0:c] for x in ball]
    att = [jnp.where(eye, jnp.sum(qi * ki, axis=-1, keepdims=True), 0.0) for qi, ki in zip(q, k)]
    for l in range(nlev):
        upper = (row_id & (2 ** (l + 1) - 1)) >= 2 ** l
        x = [(jnp.where(upper, qi, ki) * jnp.exp(-jnp.abs(bi - bl[(l + 1) * c:(l + 2) * c]))).astype(BF16)
             for qi, ki, bi, bl in zip(q, k, b, ball)]
        att = [ai + _dot_nt(xi, xi) * masks_ref[l] for ai, xi in zip(att, x)]
    s = [s_scr[n, h] for n, h in probs]
    o = [_dot(ai.astype(BF16), vi) + _dot((qi * jnp.exp(bi)).astype(BF16), si.astype(BF16))
         for ai, vi, qi, bi, si in zip(att, v, q, b, s)]
    kd = [(ki * jnp.exp(bl[(nlev + 1) * c:(nlev + 2) * c] - bi)).astype(BF16) for ki, bi, bl in zip(k, b, ball)]
    b_col = [_dot_tn(p[0], ones) + _dot_tn(p[1], ones) + _dot_tn(p[2], ones) for p in parts]
    for (n, h), bc, si, kdi, vi in zip(probs, b_col, s, kd, v):
        decay = jnp.concatenate([jnp.exp(bc)] * (dv // LANES), axis=1)
        s_scr[n, h] = decay * si + _dot_tn(kdi, vi)
    gate = r_ref[...].astype(F32)
    for h in range(heads):
        cols = slice(h * dv, (h + 1) * dv)
        outs = []
        for n in range(nb):
            oi = o[n * heads + h]
            on = oi * lax.rsqrt(jnp.mean(oi * oi, axis=-1, keepdims=True) + EPS)
            outs.append(on * gn_ref[:, cols] * gate[n * c:(n + 1) * c, cols])
        o_ref[:, cols] = (outs[0] if nb == 1 else jnp.concatenate(outs, axis=0)).astype(o_ref.dtype)

    @pl.when(chunk == pl.num_programs(1) - 1)
    def _():
        sout_ref[...] = s_scr[...]


def _gla_branch(zqkv, lg, zrgg, s0, gn, *, row0, seq_len, nb):
    nseq, heads, dk, dv = s0.shape
    gk, gv = heads * dk, heads * dv
    c = min(GLA_CHUNK, seq_len)
    nc = seq_len // c
    assert seq_len % c == 0 and (nb == 1 or nc == 1) and nseq % nb == 0
    rows = nb * c
    assert row0 % rows == 0 and (2 * gk) % gv == 0 and dv % LANES == 0
    base = row0 // rows
    mall, masks, nlev = _gla_tables(c)
    rmap = lambda s, ch: base + s * nc + ch
    return pl.pallas_call(
        functools.partial(_gla_kernel, nb=nb, heads=heads, c=c, nlev=nlev, scale=dk ** -0.5),
        grid=(nseq // nb, nc),
        in_specs=[pl.BlockSpec((rows, gk), lambda s, ch: (rmap(s, ch), 0)),
                  pl.BlockSpec((rows, gk), lambda s, ch: (rmap(s, ch), 1)),
                  pl.BlockSpec((rows, gv), lambda s, ch: (rmap(s, ch), 2 * gk // gv)),
                  pl.BlockSpec((rows, gk), lambda s, ch: (rmap(s, ch), 0)),
                  pl.BlockSpec((rows, gv), lambda s, ch: (rmap(s, ch), 0)),
                  pl.BlockSpec((nb, heads, dk, dv), lambda s, ch: (s, 0, 0, 0)),
                  pl.BlockSpec((1, gv), lambda s, ch: (0, 0)),
                  pl.BlockSpec(mall.shape, lambda s, ch: (0, 0)),
                  pl.BlockSpec(masks.shape, lambda s, ch: (0, 0, 0))],
        out_specs=[pl.BlockSpec((rows, gv), lambda s, ch: (s * nc + ch, 0)),
                   pl.BlockSpec((nb, heads, dk, dv), lambda s, ch: (s, 0, 0, 0))],
        out_shape=[jax.ShapeDtypeStruct((nseq * seq_len, gv), BF16),
                   jax.ShapeDtypeStruct((nseq, heads, dk, dv), F32)],
        scratch_shapes=[pltpu.VMEM((nb, heads, dk, dv), F32)],
        compiler_params=_params("parallel", "arbitrary"),
        name="gla_branch",
    )(zqkv, zqkv, zqkv, lg, zrgg, s0, gn.reshape(1, gv), mall, masks)


def _merge_kernel(cp_ref, cs_ref, ogp_ref, ogs_ref, wc_ref, wg_ref, ga_ref, gb_ref, o_ref, *, n_first):
    def emit(refs):
        c_ref, og_ref = refs
        a = _dot(c_ref[...].astype(BF16), wc_ref[...])
        b = _dot(og_ref[...], wg_ref[...])
        o_ref[...] = (ga_ref[...].astype(F32) * a + gb_ref[...].astype(F32) * b).astype(o_ref.dtype)

    _pick_part(n_first, (cp_ref, ogp_ref), (cs_ref, ogs_ref), emit)


def _merge(c_p, c_s, og_p, og_s, wc, wg, zrgg, ga_col0, gb_col0):
    (tp, kc), ts = c_p.shape, c_s.shape[0]
    kg = og_p.shape[1]
    n = wc.shape[1]
    tm, tn = _tile(math.gcd(tp, ts), 512), _tile(math.gcd(math.gcd(n, ga_col0), gb_col0), 1024, LANES)
    nf = tp // tm
    return pl.pallas_call(
        functools.partial(_merge_kernel, n_first=nf),
        grid=((tp + ts) // tm, n // tn),
        in_specs=[*_two_part_specs(tm, kc, nf, 2), *_two_part_specs(tm, kg, nf, 2),
                  pl.BlockSpec((kc, tn), lambda i, j: (0, j)),
                  pl.BlockSpec((kg, tn), lambda i, j: (0, j)),
                  pl.BlockSpec((tm, tn), lambda i, j: (i, ga_col0 // tn + j)),
                  pl.BlockSpec((tm, tn), lambda i, j: (i, gb_col0 // tn + j))],
        out_specs=pl.BlockSpec((tm, tn), lambda i, j: (i, j)),
        out_shape=jax.ShapeDtypeStruct((tp + ts, n), BF16),
        compiler_params=_params("parallel", "parallel"),
        name="merge_proj",
    )(c_p, c_s, og_p, og_s, wc, wg, zrgg, zrgg)


def _mix_kernel(m_ref, w_ref, xp_ref, xs_ref, g_ref, x2_ref, h2t_ref, *, n_first):
    def emit(x_ref):
        x2 = x_ref[...] + _dot(m_ref[...], w_ref[...])
        x2_ref[...] = x2
        y = x2 * lax.rsqrt(jnp.mean(x2 * x2, axis=-1, keepdims=True) + EPS)
        h2t_ref[...] = (y * g_ref[...]).T.astype(h2t_ref.dtype)

    _pick_part(n_first, xp_ref, xs_ref, emit)


def _mix(merged, w, xp, xs, g):
    (tp, d), ts = xp.shape, xs.shape[0]
    k = merged.shape[1]
    tm = _tile(math.gcd(tp, ts), 256, LANES)
    return pl.pallas_call(
        functools.partial(_mix_kernel, n_first=tp // tm),
        grid=((tp + ts) // tm,),
        in_specs=[pl.BlockSpec((tm, k), lambda i: (i, 0)),
                  pl.BlockSpec((k, d), lambda i: (0, 0)),
                  *_two_part_specs(tm, d, tp // tm),
                  pl.BlockSpec((1, d), lambda i: (0, 0))],
        out_specs=[pl.BlockSpec((tm, d), lambda i: (i, 0)), pl.BlockSpec((d, tm), lambda i: (0, i))],
        out_shape=[jax.ShapeDtypeStruct((tp + ts, d), F32), jax.ShapeDtypeStruct((d, tp + ts), BF16)],
        compiler_params=_params("parallel"),
        name="mix_proj",
    )(merged, w, xp, xs, g.reshape(1, d))


def _score_kernel(h2t_ref, wqt_ref, k1_ref, k2_ref, s_ref, *, heads, half):
    qt = _dot(wqt_ref[...], h2t_ref[...])
    k1, k2 = k1_ref[...], k2_ref[...]
    for h in range(heads):
        lo = 2 * half * h
        s_ref[2 * h] = _dot(k1, qt[lo:lo + half].astype(BF16))
        s_ref[2 * h + 1] = _dot(k2, qt[lo + half:lo + 2 * half].astype(BF16))


def _peer_scores(h2t, wqt, k1, k2, heads):
    d, t = h2t.shape
    nq = wqt.shape[0]
    nkeys, half = k1.shape
    tl = _tile(t, 512, LANES)
    return pl.pallas_call(
        functools.partial(_score_kernel, heads=heads, half=half),
        grid=(t // tl,),
        in_specs=[pl.BlockSpec((d, tl), lambda i: (0, i)),
                  pl.BlockSpec((nq, d), lambda i: (0, 0)),
                  pl.BlockSpec((nkeys, half), lambda i: (0, 0)),
                  pl.BlockSpec((nkeys, half), lambda i: (0, 0))],
        out_specs=pl.BlockSpec((2 * heads, nkeys, tl), lambda i: (0, 0, i)),
        out_shape=jax.ShapeDtypeStruct((2 * heads, nkeys, t), F32),
        compiler_params=_params("parallel"),
        name="peer_scores",
    )(h2t, wqt, k1, k2)


def _top_ranks(arrays, k, first_index_ties):
    tl = arrays[0].shape[1]
    idx = [lax.broadcasted_iota(jnp.int32, a.shape, 0).astype(F32) for a in arrays]
    slot = lax.broadcasted_iota(jnp.int32, (k, tl), 0)

    def body(r, carry):
        out = []
        for (work, rank, vals), ix in zip(carry, idx):
            m = jnp.max(work, axis=0, keepdims=True)
            if first_index_ties:
                first = jnp.min(jnp.where(work == m, ix, float(work.shape[0])), axis=0, keepdims=True)
                sel = ix == first
            else:
                sel = work == m
            rank = jnp.where(sel, lax.convert_element_type(r, F32), rank)
            work = jnp.where(sel, -jnp.inf, work)
            vals = jnp.where(slot == r, m, vals)
            out.append((work, rank, vals))
        return tuple(out)

    init = tuple((a, jnp.full(a.shape, float(k), F32), jnp.zeros((k, tl), F32)) for a in arrays)
    return [(rank, vals) for _, rank, vals in lax.fori_loop(0, k, body, init)]


def _all_ranked(rank, k):
    n = jnp.sum(jnp.where(rank < float(k), 1.0, 0.0), axis=0, keepdims=True)
    return jnp.where(n == float(k), 1.0, 0.0)


def _candidate_tables(k):
    pairs = [(r, c) for r in range(k) for c in range(k) if (r + 1) * (c + 1) <= k]
    rows = -(-len(pairs) // 8) * 8
    sel_r = np.zeros((rows, k), np.float32)
    sel_c = np.zeros((rows, k), np.float32)
    pad = np.zeros((rows, LANES), np.float32)
    for i, (r, c) in enumerate(pairs):
        sel_r[i, r] = 1.0
        sel_c[i, c] = 1.0
    pad[len(pairs):] = -np.inf
    return (jnp.asarray(sel_r, BF16), jnp.asarray(sel_c, BF16), jnp.asarray(sel_r.T.copy(), BF16), jnp.asarray(pad))


def _pick(sel, x):
    hi, mid, lo = _split3(x)
    return _dot(sel, hi) + _dot(sel, mid) + _dot(sel, lo)


def _select_kernel(s_ref, selr_ref, selc_ref, selrt_ref, pad_ref, rank2_ref, cnt1_ref, p1_ref, p2_ref, *, heads, topk, group):
    def run(first_index_ties):
        def per_group(g, ok):
            hs = [g * group + i for i in range(group)]
            s1 = [s_ref[2 * h] for h in hs]
            s2 = [s_ref[2 * h + 1] for h in hs]
            ranked = [_top_ranks([a, b], topk, first_index_ties) for a, b in zip(s1, s2)]
            (rank1, v1), (rank2, v2) = zip(*[r[0] for r in ranked]), zip(*[r[1] for r in ranked])
            cand = [_pick(selr_ref[...], a) + _pick(selc_ref[...], b) + pad_ref[...] for a, b in zip(v1, v2)]
            rank_c, cs = zip(*_top_ranks(cand, topk, first_index_ties))
            for i, h in enumerate(hs):
                taken = jnp.where(rank_c[i] < float(topk), 1.0, 0.0).astype(BF16)
                per_row = _dot(selrt_ref[...], taken)
                z = jnp.sum(jnp.exp(cs[i] - cs[i][0:1]), axis=0, keepdims=True)
                cnt1 = jnp.zeros_like(s1[i])
                for r in range(topk):
                    cnt1 = cnt1 + jnp.where(rank1[i] == float(r), per_row[r:r + 1], 0.0)
                rank2_ref[h, 0] = rank2[i]
                cnt1_ref[h, 0] = cnt1
                p1_ref[h, 0] = jnp.exp(s1[i] - v1[i][0:1]) / z
                p2_ref[h, 0] = jnp.exp(s2[i] - v2[i][0:1])
                if not first_index_ties:
                    ok = ok * _all_ranked(rank1[i], topk) * _all_ranked(rank2[i], topk) * _all_ranked(rank_c[i], topk)
            return ok

        return lax.fori_loop(0, heads // group, per_group, jnp.ones((1, s_ref.shape[2]), F32))

    ok = run(False)

    @pl.when(jnp.min(ok) < 0.5)
    def _():
        run(True)


def _peer_select(scores, heads):
    _, nkeys, t = scores.shape
    assert t % LANES == 0
    shape = lambda dt: jax.ShapeDtypeStruct((heads, t // LANES, nkeys, LANES), dt)
    spec = pl.BlockSpec((heads, 1, nkeys, LANES), lambda i: (0, i, 0, 0))
    tables = _candidate_tables(PEER_TOPK)
    return pl.pallas_call(
        functools.partial(_select_kernel, heads=heads, topk=PEER_TOPK, group=math.gcd(heads, 4)),
        grid=(t // LANES,),
        in_specs=[pl.BlockSpec((2 * heads, nkeys, LANES), lambda i: (0, 0, i))]
        + [pl.BlockSpec(tb.shape, lambda i: (0, 0)) for tb in tables],
        out_specs=[spec] * 4,
        out_shape=[shape(F32)] * 4,
        compiler_params=_params("parallel"),
        name="peer_select",
    )(scores, *tables)


def _peer_kernel(h2t_ref, u_ref, vt_ref, rank2_ref, cnt1_ref, p1_ref, p2_ref, x2_ref, fg_ref, yp_ref, ys_ref, acc_ref,
                 act0_ref, act1_ref, w0_ref, w1_ref, *, heads, nkeys, n1, n_sub, per, n_first):
    s = pl.program_id(1)
    last = pl.num_programs(1) - 1

    @pl.when(s == 0)
    def _():
        acc_ref[...] = jnp.zeros_like(acc_ref)
        act1_ref[...] = jnp.zeros_like(act1_ref)
        w0_ref[...] = jnp.zeros_like(w0_ref)

    def step(act_prev, w_next, act_next, w_prev):
        nq = act_prev.shape[1] // LANES
        d, dm = acc_ref.shape[0], u_ref.shape[1]
        sub = n1 // n_sub

        def gate_piece(q, j):
            rows, lanes = slice(j * nkeys, (j + 1) * nkeys), slice(q * LANES, (q + 1) * LANES)
            row = j if per == 1 else (jnp.clip(s - 1, 0, last - 2) % per) * n1 + j
            g = None
            for h in range(heads):
                c1 = cnt1_ref[h, q, pl.ds(row, 1), :]
                w1 = p1_ref[h, q, pl.ds(row, 1), :]
                term = jnp.where(rank2_ref[h, q] < c1, p2_ref[h, q], 0.0) * w1
                g = term if g is None else g + term
            w_next[rows, lanes] = (g * jax.nn.gelu(act_prev[rows, lanes])).astype(BF16)

        def act_piece(m, k):
            es = slice(m * sub * nkeys, (m + 1) * sub * nkeys)
            ks = slice(k * dm // nq, (k + 1) * dm // nq)
            part = _dot(u_ref[es, ks], h2t_ref[ks, :])
            if k == 0:
                act_next[es, :] = part
            else:
                act_next[es, :] += part

        def out_piece(m, r):
            es = slice(m * sub * nkeys, (m + 1) * sub * nkeys)
            rs = slice(r * d // nq, (r + 1) * d // nq)
            acc_ref[rs, :] += _dot(vt_ref[rs, es], w_prev[es, :])

        for m in range(n_sub):
            for q in range(nq):
                for jj in range(sub):
                    if jj == sub // 4:
                        act_piece(m, q)
                    if jj == (3 * sub) // 4:
                        out_piece(m, q)
                    gate_piece(q, m * sub + jj)

    @pl.when(s % 2 == 0)
    def _():
        step(act1_ref, w1_ref, act0_ref, w0_ref)

    @pl.when(s % 2 == 1)
    def _():
        step(act0_ref, w0_ref, act1_ref, w1_ref)

    def finish(y_ref):
        y = x2_ref[...] + acc_ref[...].T
        y_ref[...] = y * lax.rsqrt(jnp.mean(y * y, axis=-1, keepdims=True) + EPS) * fg_ref[...]

    @pl.when(s == last)
    def _():
        _pick_part(n_first, yp_ref, ys_ref, finish)


def _peer_dense(h2t, u, vt, rank2, cnt1, p1, p2, x2, fg, t_first):
    d, t = h2t.shape
    n_exp = u.shape[0]
    heads, _, nkeys, _ = rank2.shape
    tb = _tile(math.gcd(t_first, t - t_first), 512, LANES)
    assert tb % LANES == 0
    nf = t_first // tb
    n1 = max(1, min(512, n_exp) // nkeys)
    n_sub = max(1, n1 * nkeys // 512)
    eb = n1 * nkeys
    ne = n_exp // eb
    per = max(1, 8 // n1)
    assert (n1 * per) % 8 == 0 and n1 % n_sub == 0 and n_exp % eb == 0 and ne % per == 0
    sel = pl.BlockSpec((heads, tb // LANES, nkeys, LANES), lambda i, e: (0, i, 0, 0))
    sel_rows = pl.BlockSpec((heads, tb // LANES, n1 * per, LANES),
                            lambda i, e: (0, i, jnp.clip(e - 1, 0, ne - 1) // per, 0))
    return pl.pallas_call(
        functools.partial(_peer_kernel, heads=heads, nkeys=nkeys, n1=n1, n_sub=n_sub, per=per, n_first=nf),
        grid=(t // tb, ne + 2),
        in_specs=[pl.BlockSpec((d, tb), lambda i, e: (0, i)),
                  pl.BlockSpec((eb, d), lambda i, e: (jnp.minimum(e, ne - 1), 0)),
                  pl.BlockSpec((d, eb), lambda i, e: (0, jnp.clip(e - 2, 0, ne - 1))),
                  sel, sel_rows, sel_rows, sel,
                  pl.BlockSpec((tb, d), lambda i, e: (i, 0), pipeline_mode=pl.Buffered(1)),
                  pl.BlockSpec((1, d), lambda i, e: (0, 0))],
        out_specs=[pl.BlockSpec((tb, d), lambda i, e: (jnp.minimum(i, nf - 1), 0), pipeline_mode=pl.Buffered(1)),
                   pl.BlockSpec((tb, d), lambda i, e: (jnp.maximum(i - nf, 0), 0), pipeline_mode=pl.Buffered(1))],
        out_shape=[jax.ShapeDtypeStruct((t_first, d), F32), jax.ShapeDtypeStruct((t - t_first, d), F32)],
        scratch_shapes=[pltpu.VMEM((d, tb), F32), pltpu.VMEM((eb, tb), F32), pltpu.VMEM((eb, tb), F32),
                        pltpu.VMEM((eb, tb), BF16), pltpu.VMEM((eb, tb), BF16)],
        compiler_params=_params("parallel", "arbitrary"),
        name="peer_dense",
    )(h2t, u, vt, rank2, cnt1, p1, p2, x2, fg.reshape(1, d))


def kernel(x_prompt, x_sample, state_conv, state_gla, norm1_g, w_in, w_a2, b_a, w_dw, b_dw, conv_ln_g, conv_ln_b, w_conv_out, gla_norm_g, w_gla_o, w_mix_out, norm2_g, peer_wq, peer_k1, peer_k2, peer_u, peer_v, final_g):
    assert w_in.shape[0] == 1, "one layer"
    bp, lp, d = x_prompt.shape
    bs, ls, _ = x_sample.shape
    tp, ts = bp * lp, bs * ls
    cc = state_conv.shape[-1]
    width = w_dw.shape[1]
    _, _, heads, dk, dv = state_gla.shape
    gk, gv = heads * dk, heads * dv
    rank = w_a2.shape[1]
    nkeys, half = peer_k1.shape[1:]
    pheads = peer_wq.shape[2] // (2 * half)

    xp, xs = x_prompt.reshape(tp, d), x_sample.reshape(ts, d)

    w = w_in[0].astype(BF16)
    o = 0
    w_ca, o = w[:, o:o + cc], o + cc
    w_cg, o = w[:, o:o + cc], o + cc
    w_qkv, o = w[:, o:o + 2 * gk + gv], o + 2 * gk + gv
    w_r, o = w[:, o:o + gv], o + gv
    w_a, o = w[:, o:o + rank], o + rank
    w_gab = w[:, o:o + 2 * d]
    rp = -(-rank // LANES) * LANES
    w_a = jnp.pad(w_a, ((0, 0), (0, rp - rank)))
    w_a2p = jnp.pad(w_a2[0], ((0, rp - rank), (0, 0)))

    h = _rmsnorm(xp, xs, norm1_g[0], BF16)
    u = _glu_proj(h, w_ca, w_cg)
    zqkv = _linear(h, w_qkv)
    zrgg = _act_linear(h, jnp.concatenate([w_r, w_gab], axis=1), gv)
    lg = _decay_proj(h, w_a, w_a2p, b_a[0])

    conv_w = (w_dw[0], b_dw[0], conv_ln_g[0], conv_ln_b[0])
    c_p = _conv_prompt(u, bp, lp, *conv_w)
    c_s = _conv_sample(u, tp, state_conv[0], *conv_w)
    tail = lambda hist, cur: jnp.concatenate([hist, cur], axis=1)[:, -(width - 1):]
    conv_p = tail(jnp.zeros((bp, width - 1, cc), F32), u[:tp].reshape(bp, lp, cc)[:, -min(lp, width - 1):])
    conv_s = tail(state_conv[0], u[tp:].reshape(bs, ls, cc))

    gla = functools.partial(_gla_branch, zqkv, lg, zrgg, gn=gla_norm_g[0])
    og_p, gla_p = gla(jnp.zeros((bp, heads, dk, dv), F32), row0=0, seq_len=lp, nb=1)
    og_s, gla_s = gla(state_gla[0], row0=tp, seq_len=ls, nb=_tile(bs, 2, 2) if ls < GLA_CHUNK else 1)
    merged = _merge(c_p, c_s, og_p, og_s, w_conv_out[0].astype(BF16), w_gla_o[0].astype(BF16), zrgg, gv, gv + d)
    x2, h2t = _mix(merged, w_mix_out[0].astype(BF16), xp, xs, norm2_g[0])

    scores = _peer_scores(h2t, peer_wq[0].T.astype(BF16), peer_k1[0].astype(BF16), peer_k2[0].astype(BF16), pheads)
    rank2, cnt1, p1, p2 = _peer_select(scores, pheads)
    y_p, y_s = _peer_dense(h2t, peer_u[0].astype(BF16), peer_v[0].T.astype(BF16), rank2, cnt1, p1, p2, x2, final_g, tp)

    return (y_p.reshape(bp, lp, d), y_s.reshape(bs, ls, d),
            conv_p[None].astype(state_conv.dtype), gla_p[None].astype(state_gla.dtype),
            conv_s[None].astype(state_conv.dtype), gla_s[None].astype(state_gla.dtype))
```

```python
import functools
import math

import numpy as np
import jax
import jax.numpy as jnp
from jax import lax
from jax.experimental import pallas as pl
from jax.experimental.pallas import tpu as pltpu

EPS = 1e-6
GLA_CHUNK = 64
GATE_TEMP = 16.0
PEER_TOPK = 16
CONV_HIST_ROWS = 32
LANES = 128
VMEM_LIMIT_BYTES = 56 * 1024 * 1024

BF16 = jnp.bfloat16
F32 = jnp.float32


def _tile(n, pref, mult=8):
    if n <= pref:
        return n
    for t in range(pref - pref % mult, 0, -mult):
        if n % t == 0:
            return t
    return n


def _params(*sem):
    return pltpu.CompilerParams(dimension_semantics=sem, vmem_limit_bytes=VMEM_LIMIT_BYTES)


def _dot(a, b):
    return jnp.dot(a, b, preferred_element_type=F32)


def _dot_tn(a, b):
    return lax.dot_general(a, b, (((0,), (0,)), ((), ())), preferred_element_type=F32)


def _dot_nt(a, b):
    return lax.dot_general(a, b, (((1,), (1,)), ((), ())), preferred_element_type=F32)


def _split3(x):
    hi = x.astype(BF16)
    r1 = x - hi.astype(F32)
    mid = r1.astype(BF16)
    lo = (r1 - mid.astype(F32)).astype(BF16)
    return hi, mid, lo


def _two_part_specs(tm, cols, n_first, grid_rank=1):
    if grid_rank == 1:
        return (pl.BlockSpec((tm, cols), lambda i: (jnp.minimum(i, n_first - 1), 0)),
                pl.BlockSpec((tm, cols), lambda i: (jnp.maximum(i - n_first, 0), 0)))
    return (pl.BlockSpec((tm, cols), lambda i, j: (jnp.minimum(i, n_first - 1), 0)),
            pl.BlockSpec((tm, cols), lambda i, j: (jnp.maximum(i - n_first, 0), 0)))


def _pick_part(n_first, first_ref, second_ref, emit):
    i = pl.program_id(0)

    @pl.when(i < n_first)
    def _():
        emit(first_ref)

    @pl.when(i >= n_first)
    def _():
        emit(second_ref)


def _rmsnorm_kernel(xp_ref, xs_ref, g_ref, o_ref, *, n_first):
    def emit(x_ref):
        x = x_ref[...]
        y = x * lax.rsqrt(jnp.mean(x * x, axis=-1, keepdims=True) + EPS)
        o_ref[...] = (y * g_ref[...]).astype(o_ref.dtype)

    _pick_part(n_first, xp_ref, xs_ref, emit)


def _rmsnorm(xp, xs, g, out_dtype):
    (tp, d), ts = xp.shape, xs.shape[0]
    tm = _tile(math.gcd(tp, ts), 512)
    return pl.pallas_call(
        functools.partial(_rmsnorm_kernel, n_first=tp // tm),
        grid=((tp + ts) // tm,),
        in_specs=[*_two_part_specs(tm, d, tp // tm), pl.BlockSpec((1, d), lambda i: (0, 0))],
        out_specs=pl.BlockSpec((tm, d), lambda i: (i, 0)),
        out_shape=jax.ShapeDtypeStruct((tp + ts, d), out_dtype),
        compiler_params=_params("parallel"),
        name="rmsnorm",
    )(xp, xs, g.reshape(1, d))


def _glu_kernel(h_ref, wa_ref, wg_ref, o_ref):
    h = h_ref[...]
    o_ref[...] = _dot(h, wa_ref[...]) * jax.nn.sigmoid(_dot(h, wg_ref[...]))


def _glu_proj(h, wa, wg):
    t, k = h.shape
    n = wa.shape[1]
    tm, tn = _tile(t, 1024), _tile(n, 512, LANES)
    return pl.pallas_call(
        _glu_kernel,
        grid=(t // tm, n // tn),
        in_specs=[pl.BlockSpec((tm, k), lambda i, j: (i, 0)),
                  pl.BlockSpec((k, tn), lambda i, j: (0, j)),
                  pl.BlockSpec((k, tn), lambda i, j: (0, j))],
        out_specs=pl.BlockSpec((tm, tn), lambda i, j: (i, j)),
        out_shape=jax.ShapeDtypeStruct((t, n), F32),
        compiler_params=_params("parallel", "parallel"),
        name="glu_proj",
    )(h, wa, wg)


def _linear_kernel(h_ref, w_ref, o_ref):
    o_ref[...] = _dot(h_ref[...], w_ref[...])


def _linear(h, w):
    t, k = h.shape
    n = w.shape[1]
    tm, tn = _tile(t, 1024), _tile(n, 1024, LANES)
    return pl.pallas_call(
        _linear_kernel,
        grid=(t // tm, n // tn),
        in_specs=[pl.BlockSpec((tm, k), lambda i, j: (i, 0)), pl.BlockSpec((k, tn), lambda i, j: (0, j))],
        out_specs=pl.BlockSpec((tm, tn), lambda i, j: (i, j)),
        out_shape=jax.ShapeDtypeStruct((t, n), F32),
        compiler_params=_params("parallel", "parallel"),
        name="qkv_proj",
    )(h, w)


def _act_linear_kernel(h_ref, w_ref, o_ref, *, n_silu_tiles):
    z = _dot(h_ref[...], w_ref[...])
    sig = jax.nn.sigmoid(z)
    o_ref[...] = jnp.where(pl.program_id(1) < n_silu_tiles, z * sig, sig).astype(o_ref.dtype)


def _act_linear(h, w, n_silu):
    t, k = h.shape
    n = w.shape[1]
    tm, tn = _tile(t, 1024), _tile(math.gcd(n, n_silu), 1024, LANES)
    return pl.pallas_call(
        functools.partial(_act_linear_kernel, n_silu_tiles=n_silu // tn),
        grid=(t // tm, n // tn),
        in_specs=[pl.BlockSpec((tm, k), lambda i, j: (i, 0)), pl.BlockSpec((k, tn), lambda i, j: (0, j))],
        out_specs=pl.BlockSpec((tm, tn), lambda i, j: (i, j)),
        out_shape=jax.ShapeDtypeStruct((t, n), BF16),
        compiler_params=_params("parallel", "parallel"),
        name="gate_proj",
    )(h, w)


def _decay_kernel(h_ref, wa_ref, wa2_ref, ba_ref, o_ref):
    za = _dot(h_ref[...], wa_ref[...])
    x = jnp.dot(za, wa2_ref[...], preferred_element_type=F32, precision=lax.Precision.HIGHEST) + ba_ref[...]
    o_ref[...] = (jnp.minimum(x, 0.0) - jnp.log1p(jnp.exp(-jnp.abs(x)))) * (1.0 / GATE_TEMP)


def _decay_proj(h, wa, wa2, ba):
    t, k = h.shape
    rp = wa.shape[1]
    n = wa2.shape[1]
    tm = _tile(t, 512)
    return pl.pallas_call(
        _decay_kernel,
        grid=(t // tm,),
        in_specs=[pl.BlockSpec((tm, k), lambda i: (i, 0)),
                  pl.BlockSpec((k, rp), lambda i: (0, 0)),
                  pl.BlockSpec((rp, n), lambda i: (0, 0)),
                  pl.BlockSpec((1, n), lambda i: (0, 0))],
        out_specs=pl.BlockSpec((tm, n), lambda i: (i, 0)),
        out_shape=jax.ShapeDtypeStruct((t, n), F32),
        compiler_params=_params("parallel"),
        name="decay_proj",
    )(h, wa, wa2, ba.reshape(1, n))


def _conv_kernel(hist_ref, cur_ref, w_ref, b_ref, g_ref, beta_ref, o_ref, ext_scr, *shift_scr,
                 nb, lt, width, rt, zero_first):
    off = CONV_HIST_ROWS - (width - 1)
    bias, gain, beta = b_ref[...], g_ref[...], beta_ref[...]
    outs = []
    for n in range(nb):
        hist = hist_ref[n]
        if zero_first:
            hist = jnp.where(pl.program_id(1) == 0, 0.0, hist)
        ext_scr[n, 0:CONV_HIST_ROWS, :] = hist
        ext_scr[n, CONV_HIST_ROWS:CONV_HIST_ROWS + lt, :] = cur_ref[n * lt:(n + 1) * lt, :]
        if shift_scr:
            span = shift_scr[0].shape[1]
            for s in range(1, 8):
                shift_scr[0][s - 1] = ext_scr[n, s:s + span, :]

        def window(start):
            if shift_scr and start % 8:
                return shift_scr[0][start % 8 - 1, start - start % 8:start - start % 8 + rt, :]
            return ext_scr[n, start:start + rt, :]

        for r0 in range(0, lt, rt):
            acc = window(r0 + off) * w_ref[0:1, :]
            for j in range(1, width):
                acc = acc + window(r0 + off + j) * w_ref[j:j + 1, :]
            c = acc + bias
            mu = jnp.mean(c, axis=-1, keepdims=True)
            d = c - mu
            var = jnp.mean(d * d, axis=-1, keepdims=True)
            y = d * lax.rsqrt(var + EPS) * gain + beta
            outs.append(y * jax.nn.sigmoid(y))
    o_ref[...] = outs[0] if len(outs) == 1 else jnp.concatenate(outs, axis=0)


def _conv_call(hist, hist_spec, u, cur_spec, grid, out_rows, w_dw, b_dw, ln_g, ln_b, nb, lt, zero_first):
    c = u.shape[1]
    width = w_dw.shape[0]
    row = lambda a: a.reshape(1, c)
    const = lambda shape: pl.BlockSpec(shape, lambda *_: (0, 0))
    out_map = (lambda b, i: (b * grid[1] + i, 0)) if len(grid) == 2 else (lambda g: (g, 0))
    return pl.pallas_call(
        functools.partial(_conv_kernel, nb=nb, lt=lt, width=width, rt=_tile(lt, 32), zero_first=zero_first),
        grid=grid,
        in_specs=[hist_spec, cur_spec, const((width, c)), const((1, c)), const((1, c)), const((1, c))],
        out_specs=pl.BlockSpec((nb * lt, c), out_map),
        out_shape=jax.ShapeDtypeStruct((out_rows, c), F32),
        scratch_shapes=[pltpu.VMEM((nb, CONV_HIST_ROWS + lt, c), F32)]
        + ([pltpu.VMEM((7, lt + CONV_HIST_ROWS - 8, c), F32)] if nb == 1 and lt >= 64 else []),
        compiler_params=_params(*(("parallel",) * len(grid))),
        name="conv_branch",
    )(hist, u, w_dw, row(b_dw), row(ln_g), row(ln_b))


def _conv_prompt(u, nseq, seq_len, w_dw, b_dw, ln_g, ln_b):
    c = u.shape[1]
    lt = _tile(seq_len, 256, CONV_HIST_ROWS)
    nt = seq_len // lt
    assert lt % CONV_HIST_ROWS == 0 and u.shape[0] % CONV_HIST_ROWS == 0
    per = lt // CONV_HIST_ROWS
    slabs = u.reshape(u.shape[0] // CONV_HIST_ROWS, CONV_HIST_ROWS, c)
    hist_spec = pl.BlockSpec((1, CONV_HIST_ROWS, c), lambda b, i: (jnp.maximum((b * nt + i) * per - 1, 0), 0, 0))
    cur_spec = pl.BlockSpec((lt, c), lambda b, i: (b * nt + i, 0))
    return _conv_call(slabs, hist_spec, u, cur_spec, (nseq, nt), nseq * seq_len, w_dw, b_dw, ln_g, ln_b, 1, lt, True)


def _conv_sample(u, row0, state, w_dw, b_dw, ln_g, ln_b):
    nseq, hw, c = state.shape
    seq_len = (u.shape[0] - row0) // nseq
    nb = _tile(nseq, 16, 1)
    hist = jnp.pad(state, ((0, 0), (CONV_HIST_ROWS - hw, 0), (0, 0)))
    assert row0 % (nb * seq_len) == 0
    base = row0 // (nb * seq_len)
    hist_spec = pl.BlockSpec((nb, CONV_HIST_ROWS, c), lambda g: (g, 0, 0))
    cur_spec = pl.BlockSpec((nb * seq_len, c), lambda g: (base + g, 0))
    return _conv_call(hist, hist_spec, u, cur_spec, (nseq // nb,), nseq * seq_len, w_dw, b_dw, ln_g, ln_b,
                      nb, seq_len, False)


def _gla_tables(c):
    nlev = int(math.log2(c))
    assert 2 ** nlev == c
    t = np.arange(c)
    tri = (t[None, :] <= t[:, None]).astype(np.float32)
    mats = [tri]
    masks = []
    for l in range(nlev):
        hs = 2 ** l
        mid = (t // (2 * hs)) * (2 * hs) + hs
        mats.append(tri[mid - 1])
        same = (t[:, None] // (2 * hs)) == (t[None, :] // (2 * hs))
        upper = (t % (2 * hs)) >= hs
        masks.append((same & upper[:, None] & ~upper[None, :]).astype(np.float32))
    mats.append(np.ones((c, c), np.float32))
    return jnp.asarray(np.concatenate(mats, axis=0), BF16), jnp.asarray(np.stack(masks)), nlev


def _gla_kernel(q_ref, k_ref, v_ref, lg_ref, r_ref, s0_ref, gn_ref, mall_ref, masks_ref, o_ref, sout_ref, s_scr,
                *, nb, heads, c, nlev, scale):
    chunk = pl.program_id(1)

    @pl.when(chunk == 0)
    def _():
        s_scr[...] = s0_ref[...]

    dk = q_ref.shape[1] // heads
    dv = v_ref.shape[1] // heads
    probs = [(n, h) for n in range(nb) for h in range(heads)]
    tile = lambda ref, n, h, w: ref[n * c:(n + 1) * c, h * w:(h + 1) * w]
    mall = mall_ref[...]
    row_id = lax.broadcasted_iota(jnp.int32, (c, dk), 0)
    eye = lax.broadcasted_iota(jnp.int32, (c, c), 0) == lax.broadcasted_iota(jnp.int32, (c, c), 1)
    ones = jnp.ones((c, LANES), BF16)

    q = [tile(q_ref, n, h, dk) * scale for n, h in probs]
    k = [tile(k_ref, n, h, dk) for n, h in probs]
    v = [tile(v_ref, n, h, dv).astype(BF16) for n, h in probs]
    parts = [_split3(tile(lg_ref, n, h, dk)) for n, h in probs]
    ball = [_dot(mall, p[0]) + _dot(mall, p[1]) + _dot(mall, p[2]) for p in parts]
    b = [x[0:c] for x in ball]
    att = [jnp.where(eye, jnp.sum(qi * ki, axis=-1, keepdims=True), 0.0) for qi, ki in zip(q, k)]
    for l in range(nlev):
        upper = (row_id & (2 ** (l + 1) - 1)) >= 2 ** l
        x = [(jnp.where(upper, qi, ki) * jnp.exp(-jnp.abs(bi - bl[(l + 1) * c:(l + 2) * c]))).astype(BF16)
             for qi, ki, bi, bl in zip(q, k, b, ball)]
        att = [ai + _dot_nt(xi, xi) * masks_ref[l] for ai, xi in zip(att, x)]
    s = [s_scr[n, h] for n, h in probs]
    o = [_dot(ai.astype(BF16), vi) + _dot((qi * jnp.exp(bi)).astype(BF16), si.astype(BF16))
         for ai, vi, qi, bi, si in zip(att, v, q, b, s)]
    kd = [(ki * jnp.exp(bl[(nlev + 1) * c:(nlev + 2) * c] - bi)).astype(BF16) for ki, bi, bl in zip(k, b, ball)]
    b_col = [_dot_tn(p[0], ones) + _dot_tn(p[1], ones) + _dot_tn(p[2], ones) for p in parts]
    for (n, h), bc, si, kdi, vi in zip(probs, b_col, s, kd, v):
        decay = jnp.concatenate([jnp.exp(bc)] * (dv // LANES), axis=1)
        s_scr[n, h] = decay * si + _dot_tn(kdi, vi)
    gate = r_ref[...].astype(F32)
    for h in range(heads):
        cols = slice(h * dv, (h + 1) * dv)
        outs = []
        for n in range(nb):
            oi = o[n * heads + h]
            on = oi * lax.rsqrt(jnp.mean(oi * oi, axis=-1, keepdims=True) + EPS)
            outs.append(on * gn_ref[:, cols] * gate[n * c:(n + 1) * c, cols])
        o_ref[:, cols] = (outs[0] if nb == 1 else jnp.concatenate(outs, axis=0)).astype(o_ref.dtype)

    @pl.when(chunk == pl.num_programs(1) - 1)
    def _():
        sout_ref[...] = s_scr[...]


def _gla_branch(zqkv, lg, zrgg, s0, gn, *, row0, seq_len, nb):
    nseq, heads, dk, dv = s0.shape
    gk, gv = heads * dk, heads * dv
    c = min(GLA_CHUNK, seq_len)
    nc = seq_len // c
    assert seq_len % c == 0 and (nb == 1 or nc == 1) and nseq % nb == 0
    rows = nb * c
    assert row0 % rows == 0 and (2 * gk) % gv == 0 and dv % LANES == 0
    base = row0 // rows
    mall, masks, nlev = _gla_tables(c)
    rmap = lambda s, ch: base + s * nc + ch
    return pl.pallas_call(
        functools.partial(_gla_kernel, nb=nb, heads=heads, c=c, nlev=nlev, scale=dk ** -0.5),
        grid=(nseq // nb, nc),
        in_specs=[pl.BlockSpec((rows, gk), lambda s, ch: (rmap(s, ch), 0)),
                  pl.BlockSpec((rows, gk), lambda s, ch: (rmap(s, ch), 1)),
                  pl.BlockSpec((rows, gv), lambda s, ch: (rmap(s, ch), 2 * gk // gv)),
                  pl.BlockSpec((rows, gk), lambda s, ch: (rmap(s, ch), 0)),
                  pl.BlockSpec((rows, gv), lambda s, ch: (rmap(s, ch), 0)),
                  pl.BlockSpec((nb, heads, dk, dv), lambda s, ch: (s, 0, 0, 0)),
                  pl.BlockSpec((1, gv), lambda s, ch: (0, 0)),
                  pl.BlockSpec(mall.shape, lambda s, ch: (0, 0)),
                  pl.BlockSpec(masks.shape, lambda s, ch: (0, 0, 0))],
        out_specs=[pl.BlockSpec((rows, gv), lambda s, ch: (s * nc + ch, 0)),
                   pl.BlockSpec((nb, heads, dk, dv), lambda s, ch: (s, 0, 0, 0))],
        out_shape=[jax.ShapeDtypeStruct((nseq * seq_len, gv), BF16),
                   jax.ShapeDtypeStruct((nseq, heads, dk, dv), F32)],
        scratch_shapes=[pltpu.VMEM((nb, heads, dk, dv), F32)],
        compiler_params=_params("parallel", "arbitrary"),
        name="gla_branch",
    )(zqkv, zqkv, zqkv, lg, zrgg, s0, gn.reshape(1, gv), mall, masks)


def _merge_kernel(cp_ref, cs_ref, ogp_ref, ogs_ref, wc_ref, wg_ref, ga_ref, gb_ref, o_ref, *, n_first):
    def emit(refs):
        c_ref, og_ref = refs
        a = _dot(c_ref[...].astype(BF16), wc_ref[...])
        b = _dot(og_ref[...], wg_ref[...])
        o_ref[...] = (ga_ref[...].astype(F32) * a + gb_ref[...].astype(F32) * b).astype(o_ref.dtype)

    _pick_part(n_first, (cp_ref, ogp_ref), (cs_ref, ogs_ref), emit)


def _merge(c_p, c_s, og_p, og_s, wc, wg, zrgg, ga_col0, gb_col0):
    (tp, kc), ts = c_p.shape, c_s.shape[0]
    kg = og_p.shape[1]
    n = wc.shape[1]
    tm, tn = _tile(math.gcd(tp, ts), 512), _tile(math.gcd(math.gcd(n, ga_col0), gb_col0), 1024, LANES)
    nf = tp // tm
    return pl.pallas_call(
        functools.partial(_merge_kernel, n_first=nf),
        grid=((tp + ts) // tm, n // tn),
        in_specs=[*_two_part_specs(tm, kc, nf, 2), *_two_part_specs(tm, kg, nf, 2),
                  pl.BlockSpec((kc, tn), lambda i, j: (0, j)),
                  pl.BlockSpec((kg, tn), lambda i, j: (0, j)),
                  pl.BlockSpec((tm, tn), lambda i, j: (i, ga_col0 // tn + j)),
                  pl.BlockSpec((tm, tn), lambda i, j: (i, gb_col0 // tn + j))],
        out_specs=pl.BlockSpec((tm, tn), lambda i, j: (i, j)),
        out_shape=jax.ShapeDtypeStruct((tp + ts, n), BF16),
        compiler_params=_params("parallel", "parallel"),
        name="merge_proj",
    )(c_p, c_s, og_p, og_s, wc, wg, zrgg, zrgg)


def _mix_kernel(m_ref, w_ref, xp_ref, xs_ref, g_ref, x2_ref, h2t_ref, *, n_first):
    def emit(x_ref):
        x2 = x_ref[...] + _dot(m_ref[...], w_ref[...])
        x2_ref[...] = x2
        y = x2 * lax.rsqrt(jnp.mean(x2 * x2, axis=-1, keepdims=True) + EPS)
        h2t_ref[...] = (y * g_ref[...]).T.astype(h2t_ref.dtype)

    _pick_part(n_first, xp_ref, xs_ref, emit)


def _mix(merged, w, xp, xs, g):
    (tp, d), ts = xp.shape, xs.shape[0]
    k = merged.shape[1]
    tm = _tile(math.gcd(tp, ts), 256, LANES)
    return pl.pallas_call(
        functools.partial(_mix_kernel, n_first=tp // tm),
        grid=((tp + ts) // tm,),
        in_specs=[pl.BlockSpec((tm, k), lambda i: (i, 0)),
                  pl.BlockSpec((k, d), lambda i: (0, 0)),
                  *_two_part_specs(tm, d, tp // tm),
                  pl.BlockSpec((1, d), lambda i: (0, 0))],
        out_specs=[pl.BlockSpec((tm, d), lambda i: (i, 0)), pl.BlockSpec((d, tm), lambda i: (0, i))],
        out_shape=[jax.ShapeDtypeStruct((tp + ts, d), F32), jax.ShapeDtypeStruct((d, tp + ts), BF16)],
        compiler_params=_params("parallel"),
        name="mix_proj",
    )(merged, w, xp, xs, g.reshape(1, d))


def _score_kernel(h2t_ref, wqt_ref, k1_ref, k2_ref, s_ref, *, heads, half):
    qt = _dot(wqt_ref[...], h2t_ref[...])
    k1, k2 = k1_ref[...], k2_ref[...]
    for h in range(heads):
        lo = 2 * half * h
        s_ref[2 * h] = _dot(k1, qt[lo:lo + half].astype(BF16))
        s_ref[2 * h + 1] = _dot(k2, qt[lo + half:lo + 2 * half].astype(BF16))


def _peer_scores(h2t, wqt, k1, k2, heads):
    d, t = h2t.shape
    nq = wqt.shape[0]
    nkeys, half = k1.shape
    tl = _tile(t, 512, LANES)
    return pl.pallas_call(
        functools.partial(_score_kernel, heads=heads, half=half),
        grid=(t // tl,),
        in_specs=[pl.BlockSpec((d, tl), lambda i: (0, i)),
                  pl.BlockSpec((nq, d), lambda i: (0, 0)),
                  pl.BlockSpec((nkeys, half), lambda i: (0, 0)),
                  pl.BlockSpec((nkeys, half), lambda i: (0, 0))],
        out_specs=pl.BlockSpec((2 * heads, nkeys, tl), lambda i: (0, 0, i)),
        out_shape=jax.ShapeDtypeStruct((2 * heads, nkeys, t), F32),
        compiler_params=_params("parallel"),
        name="peer_scores",
    )(h2t, wqt, k1, k2)


def _top_ranks(arrays, k):
    tl = arrays[0].shape[1]
    idx = [lax.broadcasted_iota(jnp.int32, a.shape, 0).astype(F32) for a in arrays]
    slot = lax.broadcasted_iota(jnp.int32, (k, tl), 0)

    def body(r, carry):
        out = []
        for (work, rank, vals), ix in zip(carry, idx):
            m = jnp.max(work, axis=0, keepdims=True)
            first = jnp.min(jnp.where(work == m, ix, float(work.shape[0])), axis=0, keepdims=True)
            sel = ix == first
            rank = jnp.where(sel, lax.convert_element_type(r, F32), rank)
            work = jnp.where(sel, -jnp.inf, work)
            vals = jnp.where(slot == r, m, vals)
            out.append((work, rank, vals))
        return tuple(out)

    init = tuple((a, jnp.full(a.shape, float(k), F32), jnp.zeros((k, tl), F32)) for a in arrays)
    return [(rank, vals) for _, rank, vals in lax.fori_loop(0, k, body, init)]


def _top_values(arrays, k):
    tl = arrays[0].shape[1]
    slot = lax.broadcasted_iota(jnp.int32, (k, tl), 0)

    def body(r, carry):
        out = []
        for work, vals in carry:
            m = jnp.max(work, axis=0, keepdims=True)
            out.append((jnp.where(work == m, -jnp.inf, work), jnp.where(slot == r, m, vals)))
        return tuple(out)

    init = tuple((a, jnp.zeros((k, tl), F32)) for a in arrays)
    return [vals for _, vals in lax.fori_loop(0, k, body, init)]


def _exactly(flags, k):
    return jnp.where(jnp.sum(flags, axis=0, keepdims=True) == float(k), 1.0, 0.0)


def _candidate_tables(k):
    pairs = [(r, c) for r in range(k) for c in range(k) if (r + 1) * (c + 1) <= k]
    rows = -(-len(pairs) // 8) * 8
    sel_r = np.zeros((rows, k), np.float32)
    sel_c = np.zeros((rows, k), np.float32)
    pad = np.zeros((rows, LANES), np.float32)
    for i, (r, c) in enumerate(pairs):
        sel_r[i, r] = 1.0
        sel_c[i, c] = 1.0
    pad[len(pairs):] = -np.inf
    return (jnp.asarray(sel_r, BF16), jnp.asarray(sel_c, BF16), jnp.asarray(sel_r.T.copy(), BF16), jnp.asarray(pad))


def _pick(sel, x):
    hi, mid, lo = _split3(x)
    return _dot(sel, hi) + _dot(sel, mid) + _dot(sel, lo)


def _select_kernel(s_ref, selr_ref, selc_ref, selrt_ref, pad_ref, rank2_ref, cnt1_ref, p1_ref, p2_ref, *, heads, topk, group):
    def candidates(v1, v2):
        return _pick(selr_ref[...], v1) + _pick(selc_ref[...], v2) + pad_ref[...]

    def emit(h, s1, s2, v1, v2, cs, taken, rank2, match1):
        per_row = _dot(selrt_ref[...], taken.astype(BF16))
        z = jnp.sum(jnp.exp(cs - cs[0:1]), axis=0, keepdims=True)
        cnt1 = jnp.zeros_like(s1)
        for r in range(topk):
            cnt1 = cnt1 + jnp.where(match1(r), per_row[r:r + 1], 0.0)
        rank2_ref[h, 0] = rank2
        cnt1_ref[h, 0] = cnt1
        p1_ref[h, 0] = jnp.exp(s1 - v1[0:1]) / z
        p2_ref[h, 0] = jnp.exp(s2 - v2[0:1])

    def tie_free(g, ok):
        hs = [g * group + i for i in range(group)]
        s1 = [s_ref[2 * h] for h in hs]
        s2 = [s_ref[2 * h + 1] for h in hs]
        tops = [_top_values([a, b], topk) for a, b in zip(s1, s2)]
        v1, v2 = [t[0] for t in tops], [t[1] for t in tops]
        cand = [candidates(a, b) for a, b in zip(v1, v2)]
        cs = _top_values(cand, topk)
        for i, h in enumerate(hs):
            rank2 = jnp.zeros_like(s2[i])
            for r in range(topk):
                rank2 = rank2 + jnp.where(v2[i][r:r + 1] > s2[i], 1.0, 0.0)
            taken = jnp.where(cand[i] >= cs[i][topk - 1:topk], 1.0, 0.0)
            emit(h, s1[i], s2[i], v1[i], v2[i], cs[i], taken, rank2, lambda r, i=i: s1[i] == v1[i][r:r + 1])
            in1 = jnp.where(s1[i] >= v1[i][topk - 1:topk], 1.0, 0.0)
            in2 = jnp.where(rank2 < float(topk), 1.0, 0.0)
            ok = ok * _exactly(in1, topk) * _exactly(in2, topk) * _exactly(taken, topk)
        return ok

    def index_ordered(g, carry):
        hs = [g * group + i for i in range(group)]
        s1 = [s_ref[2 * h] for h in hs]
        s2 = [s_ref[2 * h + 1] for h in hs]
        ranked = [_top_ranks([a, b], topk) for a, b in zip(s1, s2)]
        cand = [candidates(r[0][1], r[1][1]) for r in ranked]
        ranked_c = _top_ranks(cand, topk)
        for i, h in enumerate(hs):
            (rank1, v1), (rank2, v2) = ranked[i]
            rank_c, cs = ranked_c[i]
            taken = jnp.where(rank_c < float(topk), 1.0, 0.0)
            emit(h, s1[i], s2[i], v1, v2, cs, taken, rank2, lambda r, rank1=rank1: rank1 == float(r))
        return carry

    ok = lax.fori_loop(0, heads // group, tie_free, jnp.ones((1, s_ref.shape[2]), F32))

    @pl.when(jnp.min(ok) < 0.5)
    def _():
        lax.fori_loop(0, heads // group, index_ordered, 0)


def _peer_select(scores, heads):
    _, nkeys, t = scores.shape
    assert t % LANES == 0
    shape = lambda dt: jax.ShapeDtypeStruct((heads, t // LANES, nkeys, LANES), dt)
    spec = pl.BlockSpec((heads, 1, nkeys, LANES), lambda i: (0, i, 0, 0))
    tables = _candidate_tables(PEER_TOPK)
    return pl.pallas_call(
        functools.partial(_select_kernel, heads=heads, topk=PEER_TOPK, group=math.gcd(heads, 4)),
        grid=(t // LANES,),
        in_specs=[pl.BlockSpec((2 * heads, nkeys, LANES), lambda i: (0, 0, i))]
        + [pl.BlockSpec(tb.shape, lambda i: (0, 0)) for tb in tables],
        out_specs=[spec] * 4,
        out_shape=[shape(F32)] * 4,
        compiler_params=_params("parallel"),
        name="peer_select",
    )(scores, *tables)


def _peer_kernel(h2t_ref, u_ref, vt_ref, rank2_ref, cnt1_ref, p1_ref, p2_ref, x2_ref, fg_ref, yp_ref, ys_ref, acc_ref,
                 act0_ref, act1_ref, w0_ref, w1_ref, *, heads, nkeys, n1, n_sub, per, n_first):
    s = pl.program_id(1)
    last = pl.num_programs(1) - 1

    @pl.when(s == 0)
    def _():
        acc_ref[...] = jnp.zeros_like(acc_ref)
        act1_ref[...] = jnp.zeros_like(act1_ref)
        w0_ref[...] = jnp.zeros_like(w0_ref)

    def step(act_prev, w_next, act_next, w_prev):
        nq = act_prev.shape[1] // LANES
        d, dm = acc_ref.shape[0], u_ref.shape[1]
        sub = n1 // n_sub

        def gate_piece(q, j):
            rows, lanes = slice(j * nkeys, (j + 1) * nkeys), slice(q * LANES, (q + 1) * LANES)
            row = j if per == 1 else (jnp.clip(s - 1, 0, last - 2) % per) * n1 + j
            g = None
            for h in range(heads):
                c1 = cnt1_ref[h, q, pl.ds(row, 1), :]
                w1 = p1_ref[h, q, pl.ds(row, 1), :]
                term = jnp.where(rank2_ref[h, q] < c1, p2_ref[h, q], 0.0) * w1
                g = term if g is None else g + term
            w_next[rows, lanes] = (g * jax.nn.gelu(act_prev[rows, lanes])).astype(BF16)

        def act_piece(m, k):
            es = slice(m * sub * nkeys, (m + 1) * sub * nkeys)
            ks = slice(k * dm // nq, (k + 1) * dm // nq)
            part = _dot(u_ref[es, ks], h2t_ref[ks, :])
            if k == 0:
                act_next[es, :] = part
            else:
                act_next[es, :] += part

        def out_piece(m, r):
            es = slice(m * sub * nkeys, (m + 1) * sub * nkeys)
            rs = slice(r * d // nq, (r + 1) * d // nq)
            acc_ref[rs, :] += _dot(vt_ref[rs, es], w_prev[es, :])

        for m in range(n_sub):
            for q in range(nq):
                for jj in range(sub):
                    if jj == sub // 4:
                        act_piece(m, q)
                    if jj == (3 * sub) // 4:
                        out_piece(m, q)
                    gate_piece(q, m * sub + jj)

    @pl.when(s % 2 == 0)
    def _():
        step(act1_ref, w1_ref, act0_ref, w0_ref)

    @pl.when(s % 2 == 1)
    def _():
        step(act0_ref, w0_ref, act1_ref, w1_ref)

    def finish(y_ref):
        y = x2_ref[...] + acc_ref[...].T
        y_ref[...] = y * lax.rsqrt(jnp.mean(y * y, axis=-1, keepdims=True) + EPS) * fg_ref[...]

    @pl.when(s == last)
    def _():
        _pick_part(n_first, yp_ref, ys_ref, finish)


def _peer_dense(h2t, u, vt, rank2, cnt1, p1, p2, x2, fg, t_first):
    d, t = h2t.shape
    n_exp = u.shape[0]
    heads, _, nkeys, _ = rank2.shape
    tb = _tile(math.gcd(t_first, t - t_first), 512, LANES)
    assert tb % LANES == 0
    nf = t_first // tb
    n1 = max(1, min(512, n_exp) // nkeys)
    n_sub = max(1, n1 * nkeys // 512)
    eb = n1 * nkeys
    ne = n_exp // eb
    per = max(1, 8 // n1)
    assert (n1 * per) % 8 == 0 and n1 % n_sub == 0 and n_exp % eb == 0 and ne % per == 0
    sel = pl.BlockSpec((heads, tb // LANES, nkeys, LANES), lambda i, e: (0, i, 0, 0))
    sel_rows = pl.BlockSpec((heads, tb // LANES, n1 * per, LANES),
                            lambda i, e: (0, i, jnp.clip(e - 1, 0, ne - 1) // per, 0))
    return pl.pallas_call(
        functools.partial(_peer_kernel, heads=heads, nkeys=nkeys, n1=n1, n_sub=n_sub, per=per, n_first=nf),
        grid=(t // tb, ne + 2),
        in_specs=[pl.BlockSpec((d, tb), lambda i, e: (0, i)),
                  pl.BlockSpec((eb, d), lambda i, e: (jnp.minimum(e, ne - 1), 0)),
                  pl.BlockSpec((d, eb), lambda i, e: (0, jnp.clip(e - 2, 0, ne - 1))),
                  sel, sel_rows, sel_rows, sel,
                  pl.BlockSpec((tb, d), lambda i, e: (i, 0), pipeline_mode=pl.Buffered(1)),
                  pl.BlockSpec((1, d), lambda i, e: (0, 0))],
        out_specs=[pl.BlockSpec((tb, d), lambda i, e: (jnp.minimum(i, nf - 1), 0), pipeline_mode=pl.Buffered(1)),
                   pl.BlockSpec((tb, d), lambda i, e: (jnp.maximum(i - nf, 0), 0), pipeline_mode=pl.Buffered(1))],
        out_shape=[jax.ShapeDtypeStruct((t_first, d), F32), jax.ShapeDtypeStruct((t - t_first, d), F32)],
        scratch_shapes=[pltpu.VMEM((d, tb), F32), pltpu.VMEM((eb, tb), F32), pltpu.VMEM((eb, tb), F32),
                        pltpu.VMEM((eb, tb), BF16), pltpu.VMEM((eb, tb), BF16)],
        compiler_params=_params("parallel", "arbitrary"),
        name="peer_dense",
    )(h2t, u, vt, rank2, cnt1, p1, p2, x2, fg.reshape(1, d))


def kernel(x_prompt, x_sample, state_conv, state_gla, norm1_g, w_in, w_a2, b_a, w_dw, b_dw, conv_ln_g, conv_ln_b, w_conv_out, gla_norm_g, w_gla_o, w_mix_out, norm2_g, peer_wq, peer_k1, peer_k2, peer_u, peer_v, final_g):
    assert w_in.shape[0] == 1, "one layer"
    bp, lp, d = x_prompt.shape
    bs, ls, _ = x_sample.shape
    tp, ts = bp * lp, bs * ls
    cc = state_conv.shape[-1]
    width = w_dw.shape[1]
    _, _, heads, dk, dv = state_gla.shape
    gk, gv = heads * dk, heads * dv
    rank = w_a2.shape[1]
    nkeys, half = peer_k1.shape[1:]
    pheads = peer_wq.shape[2] // (2 * half)

    xp, xs = x_prompt.reshape(tp, d), x_sample.reshape(ts, d)

    cols = lambda o, n: w_in[0][:, o:o + n].astype(BF16)
    o = 0
    w_ca, o = cols(o, cc), o + cc
    w_cg, o = cols(o, cc), o + cc
    w_qkv, o = cols(o, 2 * gk + gv), o + 2 * gk + gv
    w_r, o = cols(o, gv), o + gv
    w_a, o = cols(o, rank), o + rank
    w_gab = cols(o, 2 * d)
    rp = -(-rank // LANES) * LANES
    w_a = jnp.pad(w_a, ((0, 0), (0, rp - rank)))
    w_a2p = jnp.pad(w_a2[0], ((0, rp - rank), (0, 0)))

    h = _rmsnorm(xp, xs, norm1_g[0], BF16)
    u = _glu_proj(h, w_ca, w_cg)
    zqkv = _linear(h, w_qkv)
    zrgg = _act_linear(h, jnp.concatenate([w_r, w_gab], axis=1), gv)
    lg = _decay_proj(h, w_a, w_a2p, b_a[0])

    conv_w = (w_dw[0], b_dw[0], conv_ln_g[0], conv_ln_b[0])
    c_p = _conv_prompt(u, bp, lp, *conv_w)
    c_s = _conv_sample(u, tp, state_conv[0], *conv_w)
    tail = lambda hist, cur: jnp.concatenate([hist, cur], axis=1)[:, -(width - 1):]
    keep = min(lp, width - 1)
    last_rows = jnp.stack([u[(b + 1) * lp - keep:(b + 1) * lp] for b in range(bp)])
    conv_p = tail(jnp.zeros((bp, width - 1, cc), F32), last_rows)
    conv_s = tail(state_conv[0], u[tp:].reshape(bs, ls, cc))

    gla = functools.partial(_gla_branch, zqkv, lg, zrgg, gn=gla_norm_g[0])
    og_p, gla_p = gla(jnp.zeros((bp, heads, dk, dv), F32), row0=0, seq_len=lp, nb=1)
    og_s, gla_s = gla(state_gla[0], row0=tp, seq_len=ls, nb=_tile(bs, 2, 2) if ls < GLA_CHUNK else 1)
    merged = _merge(c_p, c_s, og_p, og_s, w_conv_out[0].astype(BF16), w_gla_o[0].astype(BF16), zrgg, gv, gv + d)
    x2, h2t = _mix(merged, w_mix_out[0].astype(BF16), xp, xs, norm2_g[0])

    scores = _peer_scores(h2t, peer_wq[0].T.astype(BF16), peer_k1[0].astype(BF16), peer_k2[0].astype(BF16), pheads)
    rank2, cnt1, p1, p2 = _peer_select(scores, pheads)
    y_p, y_s = _peer_dense(h2t, peer_u[0].astype(BF16), peer_v[0].T.astype(BF16), rank2, cnt1, p1, p2, x2, final_g, tp)

    return (y_p.reshape(bp, lp, d), y_s.reshape(bs, ls, d),
            conv_p[None].astype(state_conv.dtype), gla_p[None].astype(state_gla.dtype),
            conv_s[None].astype(state_conv.dtype), gla_s[None].astype(state_gla.dtype))
```
